```python
import jax
import jax.numpy as jnp
from jax import lax
import numpy as np

D_MODEL = 1024
BATCH = 1
SEQ = 16384
DEPTH = 4

CTX_LEN = 256
GRID_W = 64
Q_BLOCK = 128
RET_CHUNK = 128
ROPE_BASE = 10000.0
NORM_EPS = 1e-6
N_MOD = 6

RET_HEADS = 4
RET_DK = 128
RET_DV = 256
RET_DECAY_START = 5.0
RET_BWD_OFFSET = 0.5

MLA_HEADS = 8
MLA_Q_RANK = 256
MLA_KV_RANK = 256
MLA_NOPE = 64
MLA_ROPE = 32
MLA_V = 128

GQA_HEADS = 8
GQA_KV_HEADS = 2
GQA_HD = 128

FFN_HIDDEN = ((8 * D_MODEL + 3 * 256 - 1) // (3 * 256)) * 256

RET_W = RET_HEADS * RET_DV
MLA_W = MLA_HEADS * MLA_V
GQA_W = GQA_HEADS * GQA_HD
N_BRANCH = 3

IN_SIZES = (
    RET_HEADS * RET_DK,
    RET_HEADS * RET_DK,
    RET_W,
    RET_W,
    MLA_Q_RANK,
    MLA_KV_RANK,
    MLA_ROPE,
    GQA_W,
    GQA_KV_HEADS * GQA_HD,
    GQA_KV_HEADS * GQA_HD,
    N_BRANCH * D_MODEL,
)
IN_DIM = sum(IN_SIZES)

kernel_name = 'hybrid_ret_mla_gqa_dit_trunk'


def rms_norm(x, g=None):
    xf = x.astype(jnp.float32)
    y = xf * lax.rsqrt(jnp.mean(xf * xf, axis=-1, keepdims=True) + NORM_EPS)
    if g is not None:
        y = y * g.astype(jnp.float32)
    return y.astype(x.dtype)


def modulate(x, shift, scale):
    return x * (1 + scale) + shift


def axis_angles(pos, dim):
    half = dim // 2
    inv_freq = ROPE_BASE ** (-jnp.arange(half, dtype=jnp.float32) / half)
    ang = pos.astype(jnp.float32)[:, None] * inv_freq[None, :]
    return jnp.cos(ang), jnp.sin(ang)


def rope_tables(rows, cols, dim):
    cr, sr = axis_angles(rows, dim // 2)
    cc, sc = axis_angles(cols, dim // 2)
    return (cr, sr, cc, sc)


def rotate_half(x, cos, sin):
    x1, x2 = jnp.split(x, 2, axis=-1)
    cos = cos[:, None, :]
    sin = sin[:, None, :]
    return jnp.concatenate([x1 * cos - x2 * sin, x2 * cos + x1 * sin], axis=-1)


def rope_2d(x, tabs):
    if tabs is None:
        return x
    cr, sr, cc, sc = tabs
    xr, xc = jnp.split(x.astype(jnp.float32), 2, axis=-1)
    out = jnp.concatenate([rotate_half(xr, cr, sr), rotate_half(xc, cc, sc)], axis=-1)
    return out.astype(x.dtype)


def retention_log_decay(offset):
    h = jnp.arange(RET_HEADS, dtype=jnp.float32)
    return jnp.log1p(-jnp.exp2(-(RET_DECAY_START + offset) - h))


def retention_scan(q, k, v, log_gamma, state0):
    B, T, H, dk = q.shape
    dv = v.shape[-1]
    C = RET_CHUNK
    n = T // C
    qc = q.astype(jnp.float32).reshape(B, n, C, H, dk)
    kc = k.astype(jnp.float32).reshape(B, n, C, H, dk)
    vc = v.astype(jnp.float32).reshape(B, n, C, H, dv)
    pos = jnp.arange(C, dtype=jnp.float32)
    diff = pos[:, None] - pos[None, :]
    lower = diff >= 0
    decay = jnp.where(lower[None],
                      jnp.exp(log_gamma[:, None, None] * jnp.where(lower, diff, 0.0)[None]),
                      0.0)
    scores = jnp.einsum('bnihd,bnjhd->bnhij', qc, kc) * decay
    inner = jnp.einsum('bnhij,bnjhe->bnihe', scores, vc)
    zeta = jnp.exp(log_gamma[:, None] * (C - 1 - pos)[None, :])
    upd = jnp.einsum('bnjhd,bnjhe,hj->nbhde', kc, vc, zeta)
    chunk_decay = jnp.exp(log_gamma * C)[None, :, None, None]

    def step(s, u):
        return chunk_decay * s + u, s

    s_final, s_prev = lax.scan(step, state0, upd)
    xi = jnp.exp(log_gamma[:, None] * (pos + 1)[None, :])
    cross = jnp.einsum('bnihd,nbhde,hi->bnihe', qc, s_prev, xi)
    return (inner + cross).reshape(B, T, H, dv), s_final


def retention_readout(y, g):
    B, T = y.shape[:2]
    y = rms_norm(y).reshape(B, T, RET_W)
    return (jax.nn.silu(g.astype(jnp.float32)) * y).astype(g.dtype)


def retention_branch(fl, fc, need_ctx):
    lg_f = retention_log_decay(0.0)
    lg_b = retention_log_decay(RET_BWD_OFFSET)
    B = fl['ret_q'].shape[0]
    zero = jnp.zeros((B, RET_HEADS, RET_DK, RET_DV), jnp.float32)
    rev = lambda a: a[:, ::-1]
    qc, kc, vc = fc['ret_q'], fc['ret_k'], fc['ret_v']
    q, k, v = fl['ret_q'], fl['ret_k'], fl['ret_v']
    yc_f, s_f = retention_scan(qc, kc, vc, lg_f, zero)
    yc_b, s_b = retention_scan(rev(qc), rev(kc), rev(vc), lg_b, zero)
    y_f, _ = retention_scan(q, k, v, lg_f, s_f)
    y_b, _ = retention_scan(rev(q), rev(k), rev(v), lg_b, s_b)
    y = retention_readout(y_f + rev(y_b), fl['ret_g'])
    yc = retention_readout(yc_f + rev(yc_b), fc['ret_g']) if need_ctx else None
    return y, yc


def mixer_features(p, g_mla_q, g_mla_kv, w_mla_qb, w_mla_kvb, g_gqa_q, g_gqa_k,
                   tabs_ret, tabs_mla, tabs_gqa):
    B, T, _ = p.shape
    split_at = np.cumsum(IN_SIZES)[:-1].tolist()
    (rq, rk, rv, rg, cq, ckv, kr, gq, gk, gv, gates) = jnp.split(p, split_at, axis=-1)
    f = {}
    f['ret_q'] = rope_2d(rq.reshape(B, T, RET_HEADS, RET_DK), tabs_ret)
    f['ret_k'] = rope_2d(rk.reshape(B, T, RET_HEADS, RET_DK), tabs_ret) * (RET_DK ** -0.5)
    f['ret_v'] = rv.reshape(B, T, RET_HEADS, RET_DV)
    f['ret_g'] = rg
    qh = (rms_norm(cq, g_mla_q) @ w_mla_qb).reshape(B, T, MLA_HEADS, MLA_NOPE + MLA_ROPE)
    q_nope, q_pe = jnp.split(qh, [MLA_NOPE], axis=-1)
    kvh = (rms_norm(ckv, g_mla_kv) @ w_mla_kvb).reshape(B, T, MLA_HEADS, MLA_NOPE + MLA_V)
    k_nope, f['mla_v'] = jnp.split(kvh, [MLA_NOPE], axis=-1)
    k_pe = jnp.broadcast_to(rope_2d(kr[:, :, None, :], tabs_mla), (B, T, MLA_HEADS, MLA_ROPE))
    f['mla_q'] = jnp.concatenate([q_nope, rope_2d(q_pe, tabs_mla)], axis=-1)
    f['mla_k'] = jnp.concatenate([k_nope, k_pe], axis=-1)
    f['gqa_q'] = rope_2d(rms_norm(gq.reshape(B, T, GQA_HEADS, GQA_HD), g_gqa_q), tabs_gqa)
    f['gqa_k'] = rope_2d(rms_norm(gk.reshape(B, T, GQA_KV_HEADS, GQA_HD), g_gqa_k), tabs_gqa)
    f['gqa_v'] = gv.reshape(B, T, GQA_KV_HEADS, GQA_HD)
    f['gates'] = gates.reshape(B, T, N_BRANCH, D_MODEL)
    return f


def block_attention(q, k, v, scale):
    B, L, Hk, G, d = q.shape
    e = v.shape[-1]
    nb = L // Q_BLOCK
    kf = k.astype(jnp.float32)
    vf = v.astype(jnp.float32)
    qb = jnp.moveaxis(q.reshape(B, nb, Q_BLOCK, Hk, G, d), 1, 0)

    def one_block(qblk):
        s = jnp.einsum('bqhgd,bshd->bhgqs', qblk.astype(jnp.float32), kf) * scale
        pr = jax.nn.softmax(s, axis=-1)
        return jnp.einsum('bhgqs,bshe->bqhge', pr, vf).astype(q.dtype)

    o = lax.map(one_block, qb)
    return jnp.moveaxis(o, 0, 1).reshape(B, L, Hk * G * e)


def merge_branches(y_ret, y_mla, y_gqa, gates, w_ret_o, w_mla_o, w_gqa_o, w_out):
    gs = jax.nn.sigmoid(gates)
    z = (gs[:, :, 0] * (y_ret @ w_ret_o)
         + gs[:, :, 1] * (y_mla @ w_mla_o)
         + gs[:, :, 2] * (y_gqa @ w_gqa_o))
    return z @ w_out


def token_mixer(h, hc, w_in, g_mla_q, g_mla_kv, w_mla_qb, w_mla_kvb, g_gqa_q, g_gqa_k,
                w_ret_o, w_mla_o, w_gqa_o, w_out, tabs_ret, tabs_mla, tabs_gqa, need_ctx):
    B, L = h.shape[:2]
    Cn = hc.shape[1]
    G = GQA_HEADS // GQA_KV_HEADS
    mla_scale = (MLA_NOPE + MLA_ROPE) ** -0.5
    gqa_scale = GQA_HD ** -0.5
    fl = mixer_features(h @ w_in, g_mla_q, g_mla_kv, w_mla_qb, w_mla_kvb, g_gqa_q, g_gqa_k,
                        tabs_ret, tabs_mla, tabs_gqa)
    fc = mixer_features(hc @ w_in, g_mla_q, g_mla_kv, w_mla_qb, w_mla_kvb, g_gqa_q, g_gqa_k,
                        None, None, None)
    y_ret, yc_ret = retention_branch(fl, fc, need_ctx)
    mla_k = jnp.concatenate([fc['mla_k'], fl['mla_k']], axis=1)
    mla_v = jnp.concatenate([fc['mla_v'], fl['mla_v']], axis=1)
    y_mla = block_attention(fl['mla_q'][:, :, :, None, :], mla_k, mla_v, mla_scale)
    gqa_k = jnp.concatenate([fc['gqa_k'], fl['gqa_k']], axis=1)
    gqa_v = jnp.concatenate([fc['gqa_v'], fl['gqa_v']], axis=1)
    y_gqa = block_attention(fl['gqa_q'].reshape(B, L, GQA_KV_HEADS, G, GQA_HD),
                            gqa_k, gqa_v, gqa_scale)
    y = merge_branches(y_ret, y_mla, y_gqa, fl['gates'], w_ret_o, w_mla_o, w_gqa_o, w_out)
    if not need_ctx:
        return y, None
    yc_mla = block_attention(fc['mla_q'][:, :, :, None, :], fc['mla_k'], fc['mla_v'], mla_scale)
    yc_gqa = block_attention(fc['gqa_q'].reshape(B, Cn, GQA_KV_HEADS, G, GQA_HD),
                             fc['gqa_k'], fc['gqa_v'], gqa_scale)
    yc = merge_branches(yc_ret, yc_mla, yc_gqa, fc['gates'], w_ret_o, w_mla_o, w_gqa_o, w_out)
    return y, yc


def swiglu(h, w_ffn_in, w_ffn_out):
    a, b = jnp.split(h @ w_ffn_in, 2, axis=-1)
    return (jax.nn.silu(a) * b) @ w_ffn_out


def setup_inputs(seed: int = 0) -> dict:
    key = jax.random.key(seed)
    ks = jax.random.split(key, 22)
    f32 = jnp.float32
    D = D_MODEL

    def normal(k, shape, s=1.0):
        return jax.random.normal(k, shape, f32) * s

    def dense(k, shape, fan_in, gain=1.0):
        return jax.random.normal(k, shape, f32) * (gain * fan_in ** -0.5)

    def norm_gain(k, shape):
        return 1.0 + 0.02 * jax.random.normal(k, shape, f32)

    return {
        'x': normal(ks[0], (BATCH, SEQ, D)),
        'c': normal(ks[1], (BATCH, D)),
        'ctx': normal(ks[2], (BATCH, CTX_LEN, D)),
        'c_ctx': normal(ks[3], (D,)),
        'w_mod': dense(ks[4], (DEPTH, D, N_MOD * D), D, 0.5),
        'b_mod': normal(ks[5], (DEPTH, N_MOD * D), 0.02),
        'g_mix': norm_gain(ks[6], (DEPTH, D)),
        'w_in': dense(ks[7], (DEPTH, D, IN_DIM), D),
        'g_mla_q': norm_gain(ks[8], (DEPTH, MLA_Q_RANK)),
        'g_mla_kv': norm_gain(ks[9], (DEPTH, MLA_KV_RANK)),
        'w_mla_qb': dense(ks[10], (DEPTH, MLA_Q_RANK, MLA_HEADS * (MLA_NOPE + MLA_ROPE)), MLA_Q_RANK),
        'w_mla_kvb': dense(ks[11], (DEPTH, MLA_KV_RANK, MLA_HEADS * (MLA_NOPE + MLA_V)), MLA_KV_RANK),
        'g_gqa_q': norm_gain(ks[12], (DEPTH, GQA_HD)),
        'g_gqa_k': norm_gain(ks[13], (DEPTH, GQA_HD)),
        'w_ret_o': dense(ks[14], (DEPTH, RET_W, D), RET_W),
        'w_mla_o': dense(ks[15], (DEPTH, MLA_W, D), MLA_W),
        'w_gqa_o': dense(ks[16], (DEPTH, GQA_W, D), GQA_W),
        'w_out': dense(ks[17], (DEPTH, D, D), D),
        'g_ffn': norm_gain(ks[18], (DEPTH, D)),
        'w_ffn_in': dense(ks[19], (DEPTH, D, 2 * FFN_HIDDEN), D),
        'w_ffn_out': dense(ks[20], (DEPTH, FFN_HIDDEN, D), FFN_HIDDEN),
        'g_final': norm_gain(ks[21], (D,)),
    }


def reference(x, c, ctx, c_ctx, w_mod, b_mod, g_mix, w_in, g_mla_q, g_mla_kv, w_mla_qb,
              w_mla_kvb, g_gqa_q, g_gqa_k, w_ret_o, w_mla_o, w_gqa_o, w_out, g_ffn,
              w_ffn_in, w_ffn_out, g_final):
    B, L, D = x.shape
    ROWS = L // GRID_W
    rows = jnp.repeat(jnp.arange(ROWS, dtype=jnp.int32), GRID_W)
    cols = jnp.arange(ROWS * GRID_W, dtype=jnp.int32) % GRID_W
    tabs_ret = rope_tables(rows, cols, RET_DK)
    tabs_mla = rope_tables(rows, cols, MLA_ROPE)
    tabs_gqa = rope_tables(rows, cols, GQA_HD)
    sc = jax.nn.silu(c)
    scc = jax.nn.silu(c_ctx)
    xc = ctx
    for l in range(DEPTH):
        need_ctx = l < DEPTH - 1
        mod = (sc @ w_mod[l] + b_mod[l]).reshape(B, 1, N_MOD, D)
        modc = (scc @ w_mod[l] + b_mod[l]).reshape(1, 1, N_MOD, D)
        h = modulate(rms_norm(x, g_mix[l]), mod[:, :, 0], mod[:, :, 1])
        hc = modulate(rms_norm(xc, g_mix[l]), modc[:, :, 0], modc[:, :, 1])
        y, yc = token_mixer(h, hc, w_in[l], g_mla_q[l], g_mla_kv[l], w_mla_qb[l], w_mla_kvb[l],
                            g_gqa_q[l], g_gqa_k[l], w_ret_o[l], w_mla_o[l], w_gqa_o[l], w_out[l],
                            tabs_ret, tabs_mla, tabs_gqa, need_ctx)
        x = x + mod[:, :, 2] * y
        h2 = modulate(rms_norm(x, g_ffn[l]), mod[:, :, 3], mod[:, :, 4])
        x = x + mod[:, :, 5] * swiglu(h2, w_ffn_in[l], w_ffn_out[l])
        if need_ctx:
            xc = xc + modc[:, :, 2] * yc
            hc2 = modulate(rms_norm(xc, g_ffn[l]), modc[:, :, 3], modc[:, :, 4])
            xc = xc + modc[:, :, 5] * swiglu(hc2, w_ffn_in[l], w_ffn_out[l])
    return rms_norm(x, g_final)
```

```python
import functools
import math

import jax
import jax.numpy as jnp
import numpy as np
from jax import lax
from jax.experimental import pallas as pl
from jax.experimental.pallas import tpu as pltpu

D_MODEL = 1024
GRID_W = 64
RET_CHUNK = 128
ROPE_BASE = 10000.0
NORM_EPS = 1e-6
N_MOD = 6

RET_HEADS = 4
RET_DK = 128
RET_DV = 256
RET_DECAY_START = 5.0
RET_BWD_OFFSET = 0.5

MLA_HEADS = 8
MLA_Q_RANK = 256
MLA_KV_RANK = 256
MLA_NOPE = 64
MLA_ROPE = 32
MLA_V = 128

GQA_HEADS = 8
GQA_KV_HEADS = 2
GQA_HD = 128

RET_W = RET_HEADS * RET_DV
MLA_W = MLA_HEADS * MLA_V
GQA_W = GQA_HEADS * GQA_HD
N_BRANCH = 3

IN_SIZES = (
    RET_HEADS * RET_DK, RET_HEADS * RET_DK, RET_W, RET_W,
    MLA_Q_RANK, MLA_KV_RANK, MLA_ROPE,
    GQA_W, GQA_KV_HEADS * GQA_HD, GQA_KV_HEADS * GQA_HD,
    N_BRANCH * D_MODEL,
)

HEAD_LANES = 128
MOD_ROWS = 8
ROW_TILE = 256
V7X_VMEM_LIMIT = 56 * 1024 * 1024
LOG2E = math.log2(math.e)
NEG_BIG = -1e30

F32 = jnp.float32
BF16 = jnp.bfloat16


def _params(*sem):
    return pltpu.CompilerParams(dimension_semantics=sem, vmem_limit_bytes=V7X_VMEM_LIMIT)


def _const_spec(shape):
    zeros = (0,) * len(shape)
    return pl.BlockSpec(shape, lambda *_: zeros)


def _dot(a, b):
    return jnp.dot(a, b, preferred_element_type=F32)


def _rms(x):
    return x * lax.rsqrt(jnp.mean(x * x, axis=-1, keepdims=True) + NORM_EPS)


def _silu(x):
    return x * jax.nn.sigmoid(x)


def _rope(x, c, sa, sb, block):
    up = pltpu.roll(x, HEAD_LANES - block, 1)
    dn = pltpu.roll(x, block, 1)
    return x * c + up * sa + dn * sb


def _head(x, h):
    return x[:, h * HEAD_LANES:(h + 1) * HEAD_LANES]


def _mod_kernel(cond_ref, w_ref, b_ref, o_ref):
    s = _silu(cond_ref[...]).astype(BF16)
    o_ref[0] = _dot(s, w_ref[0].astype(BF16)) + b_ref[0]


def _mod_call(cond, w_mod, b_mod):
    depth, d, n = w_mod.shape
    tn = n // 4
    return pl.pallas_call(
        _mod_kernel,
        out_shape=jax.ShapeDtypeStruct((depth, MOD_ROWS, n), F32),
        grid=(depth, n // tn),
        in_specs=[
            pl.BlockSpec((MOD_ROWS, d), lambda l, j: (0, 0)),
            pl.BlockSpec((1, d, tn), lambda l, j: (l, 0, j)),
            pl.BlockSpec((1, 1, tn), lambda l, j: (l, 0, j)),
        ],
        out_specs=pl.BlockSpec((1, MOD_ROWS, tn), lambda l, j: (l, 0, j)),
        compiler_params=_params("arbitrary", "arbitrary"),
        name="mod",
    )(cond, w_mod, b_mod.reshape(depth, 1, n))


def _prenorm(x, mod, g, shift_row, scale_row):
    h = _rms(x) * g
    return h * (1.0 + mod[scale_row:scale_row + 1]) + mod[shift_row:shift_row + 1]


def _inproj_kernel(x_ref, mod_ref, gmix_ref, c_ref, sa_ref, sb_ref, cm_ref, sam_ref, sbm_ref,
                   w_rq, w_rk, w_rv, w_rg, w_cq, w_ckv, w_kr, w_gq, w_gk, w_gv,
                   g_mq, g_mkv, w_qb, w_kb, w_vb, g_gq, g_gk,
                   rq_o, rk_o, rv_o, rg_o, mqT_o, mk_o, mvT_o, gqT_o, gk_o, gvT_o):
    hb = _prenorm(x_ref[...], mod_ref[0], gmix_ref[...], 0, 1).astype(BF16)
    c, sa, sb = c_ref[...], sa_ref[...], sb_ref[...]
    cm, sam, sbm = cm_ref[...], sam_ref[...], sbm_ref[...]
    rope_big = lambda v: _rope(v, c, sa, sb, RET_DK // 4)
    rope_mla = lambda v: _rope(v, cm, sam, sbm, MLA_ROPE // 4)

    rq = _dot(hb, w_rq[...])
    rk = _dot(hb, w_rk[...])
    for h in range(RET_HEADS):
        sl = slice(h * HEAD_LANES, (h + 1) * HEAD_LANES)
        rq_o[:, sl] = rope_big(_head(rq, h))
        rk_o[:, sl] = rope_big(_head(rk, h)) * (RET_DK ** -0.5)
    rv_o[...] = _dot(hb, w_rv[...]).astype(BF16)
    rg_o[...] = _dot(hb, w_rg[...])

    mla_qscale = (MLA_NOPE + MLA_ROPE) ** -0.5 * LOG2E
    cqn = (_rms(_dot(hb, w_cq[...])) * g_mq[...]).astype(BF16)
    q = _dot(cqn, w_qb[...])
    for h in range(MLA_HEADS):
        mqT_o[h] = (rope_mla(_head(q, h)) * mla_qscale).T.astype(BF16)
    ckvn = (_rms(_dot(hb, w_ckv[...])) * g_mkv[...]).astype(BF16)
    kn = _dot(ckvn, w_kb[...])
    kr = rope_mla(_dot(hb, w_kr[...]))
    v = _dot(ckvn, w_vb[...])
    for h in range(MLA_HEADS):
        sl = slice(h * HEAD_LANES, (h + 1) * HEAD_LANES)
        mk_o[:, sl] = (_head(kn, h) + kr).astype(BF16)
        mvT_o[h] = _head(v, h).T.astype(BF16)

    gqa_qscale = GQA_HD ** -0.5 * LOG2E
    gq = _dot(hb, w_gq[...])
    for h in range(GQA_HEADS):
        qh = rope_big(_rms(_head(gq, h)) * g_gq[...]) * gqa_qscale
        gqT_o[h] = qh.T.astype(BF16)
    gk = _dot(hb, w_gk[...])
    gv = _dot(hb, w_gv[...])
    for h in range(GQA_KV_HEADS):
        sl = slice(h * HEAD_LANES, (h + 1) * HEAD_LANES)
        gk_o[:, sl] = rope_big(_rms(_head(gk, h)) * g_gk[...]).astype(BF16)
        gvT_o[h] = _head(gv, h).T.astype(BF16)


def _inproj_call(xs, mod, gmix, tabs, wts, n_lat_tiles):
    T, D = xs.shape
    tm = ROW_TILE
    row = lambda w: pl.BlockSpec((tm, w), lambda i: (i, 0))
    headsT = lambda nh: pl.BlockSpec((nh, HEAD_LANES, tm), lambda i: (0, 0, i))
    in_specs = [
        row(D),
        pl.BlockSpec((1, MOD_ROWS, D), lambda i: (i // n_lat_tiles, 0, 0)),
        _const_spec((1, D)),
    ] + [row(HEAD_LANES)] * 6 + [_const_spec(w.shape) for w in wts]
    out_shape = (
        jax.ShapeDtypeStruct((T, RET_HEADS * RET_DK), F32),
        jax.ShapeDtypeStruct((T, RET_HEADS * RET_DK), F32),
        jax.ShapeDtypeStruct((T, RET_W), BF16),
        jax.ShapeDtypeStruct((T, RET_W), F32),
        jax.ShapeDtypeStruct((MLA_HEADS, HEAD_LANES, T), BF16),
        jax.ShapeDtypeStruct((T, MLA_HEADS * HEAD_LANES), BF16),
        jax.ShapeDtypeStruct((MLA_HEADS, HEAD_LANES, T), BF16),
        jax.ShapeDtypeStruct((GQA_HEADS, HEAD_LANES, T), BF16),
        jax.ShapeDtypeStruct((T, GQA_KV_HEADS * GQA_HD), BF16),
        jax.ShapeDtypeStruct((GQA_KV_HEADS, HEAD_LANES, T), BF16),
    )
    out_specs = (
        row(RET_HEADS * RET_DK), row(RET_HEADS * RET_DK), row(RET_W), row(RET_W),
        headsT(MLA_HEADS), row(MLA_HEADS * HEAD_LANES), headsT(MLA_HEADS),
        headsT(GQA_HEADS), row(GQA_KV_HEADS * GQA_HD), headsT(GQA_KV_HEADS),
    )
    return pl.pallas_call(
        _inproj_kernel,
        out_shape=out_shape,
        grid=(T // tm,),
        in_specs=in_specs,
        out_specs=out_specs,
        compiler_params=_params("arbitrary"),
        name="inproj",
    )(xs, mod, gmix, *tabs, *wts)


def _ret_kernel(qf_ref, kf_ref, vf_ref, qb_ref, kb_ref, vb_ref,
                dec_ref, xi_ref, zeta_ref, cd_ref,
                yf_ref, yb_ref, sf_ref, sb_ref):
    @pl.when(pl.program_id(0) == 0)
    def _():
        sf_ref[...] = jnp.zeros_like(sf_ref)
        sb_ref[...] = jnp.zeros_like(sb_ref)

    for d, (q_ref, k_ref, v_ref, y_ref, s_ref) in enumerate(
            ((qf_ref, kf_ref, vf_ref, yf_ref, sf_ref), (qb_ref, kb_ref, vb_ref, yb_ref, sb_ref))):
        for h in range(RET_HEADS):
            ksl = slice(h * RET_DK, (h + 1) * RET_DK)
            vsl = slice(h * RET_DV, (h + 1) * RET_DV)
            q = q_ref[:, ksl]
            k = k_ref[:, ksl]
            v = v_ref[:, vsl]
            qb16 = q.astype(BF16)
            scores = lax.dot_general(qb16, k.astype(BF16), (((1,), (1,)), ((), ())),
                                     preferred_element_type=F32) * dec_ref[d, h]
            inner = _dot(scores.astype(BF16), v)
            state = s_ref[h]
            cross = _dot(qb16, state.astype(BF16)) * xi_ref[d, h]
            y_ref[:, vsl] = inner + cross
            kz = (k * zeta_ref[d, h]).T.astype(BF16)
            s_ref[h] = cd_ref[d, h] * state + _dot(kz, v)


def _ret_call(rq, rk, rv, consts, n_lat, n_ctx):
    T = rq.shape[0]
    C = RET_CHUNK
    n = n_lat + n_ctx

    def fwd(i):
        return jnp.where(i < n_ctx, n_lat + i, i - n_ctx)

    def bwd(i):
        return jnp.where(i < n_ctx, n - 1 - i, n - 1 - i)

    kw, vw = RET_HEADS * RET_DK, RET_W
    spec = lambda w, order: pl.BlockSpec((C, w), lambda i: (order(i), 0))
    dec, xi, zeta, cd = consts
    return pl.pallas_call(
        _ret_kernel,
        out_shape=(jax.ShapeDtypeStruct((T, vw), F32), jax.ShapeDtypeStruct((T, vw), F32)),
        grid=(n,),
        in_specs=[spec(kw, fwd), spec(kw, fwd), spec(vw, fwd),
                  spec(kw, bwd), spec(kw, bwd), spec(vw, bwd),
                  _const_spec(dec.shape), _const_spec(xi.shape),
                  _const_spec(zeta.shape), _const_spec(cd.shape)],
        out_specs=(spec(vw, fwd), spec(vw, bwd)),
        scratch_shapes=[pltpu.VMEM((RET_HEADS, RET_DK, RET_DV), F32),
                        pltpu.VMEM((RET_HEADS, RET_DK, RET_DV), F32)],
        compiler_params=_params("arbitrary"),
        name="retention",
    )(rq, rk, rv, rq, rk, rv, dec, xi, zeta, cd)


def _ret_consts():
    C = RET_CHUNK
    h = jnp.arange(RET_HEADS, dtype=F32)
    pos = jnp.arange(C, dtype=F32)
    diff = pos[:, None] - pos[None, :]

    def one(offset, reverse):
        lg = jnp.log1p(-jnp.exp2(-(RET_DECAY_START + offset) - h))
        dd = -diff if reverse else diff
        keep = dd >= 0
        dec = jnp.where(keep[None], jnp.exp(lg[:, None, None] * jnp.where(keep, dd, 0.0)[None]), 0.0)
        p = (C - 1 - pos) if reverse else pos
        xi = jnp.exp(lg[:, None] * (p + 1)[None, :])
        zeta = jnp.exp(lg[:, None] * (C - 1 - p)[None, :])
        cd = jnp.exp(lg * C)
        return dec, xi[:, :, None], zeta[:, :, None], jnp.broadcast_to(cd[:, None, None], (RET_HEADS, 1, 1))

    f = one(0.0, False)
    b = one(RET_BWD_OFFSET, True)
    return tuple(jnp.stack([a, c]) for a, c in zip(f, b))


def _flash_kernel(qT_ref, k_ref, vT_ref, o_ref, m_ref, l_ref, acc_ref, *, tk, nk):
    q = qT_ref[0]
    m_ref[...] = jnp.full_like(m_ref, NEG_BIG)
    l_ref[...] = jnp.zeros_like(l_ref)
    acc_ref[...] = jnp.zeros_like(acc_ref)

    def body(j, carry):
        off = pl.multiple_of(j * tk, tk)
        s = _dot(k_ref[pl.ds(off, tk), :], q)
        m_prev = m_ref[...]
        m_new = jnp.maximum(m_prev, jnp.max(s, axis=0, keepdims=True))
        alpha = jnp.exp2(m_prev - m_new)
        p = jnp.exp2(s - m_new)
        l_ref[...] = alpha * l_ref[...] + jnp.sum(p, axis=0, keepdims=True)
        pv = _dot(vT_ref[0, :, pl.ds(off, tk)], p.astype(BF16))
        acc_ref[...] = alpha * acc_ref[...] + pv
        m_ref[...] = m_new
        return carry

    lax.fori_loop(0, nk, body, 0)
    o_ref[...] = (acc_ref[...] / l_ref[...]).T.astype(o_ref.dtype)


def _flash_call(qT, k, vT, *, group, q_rows, q_off, kv_rows, kv_off, tq, tk, name):
    hq = qT.shape[0]
    nq, nk = q_rows // tq, kv_rows // tk
    qb, kb = q_off // tq, kv_off // kv_rows
    return pl.pallas_call(
        functools.partial(_flash_kernel, tk=tk, nk=nk),
        out_shape=jax.ShapeDtypeStruct((q_rows, hq * HEAD_LANES), BF16),
        grid=(hq, nq),
        in_specs=[
            pl.BlockSpec((1, HEAD_LANES, tq), lambda h, i: (h, 0, i + qb)),
            pl.BlockSpec((kv_rows, HEAD_LANES), lambda h, i: (kb, h // group)),
            pl.BlockSpec((1, HEAD_LANES, kv_rows), lambda h, i: (h // group, 0, kb)),
        ],
        out_specs=pl.BlockSpec((tq, HEAD_LANES), lambda h, i: (i, h)),
        scratch_shapes=[pltpu.VMEM((1, tq), F32), pltpu.VMEM((1, tq), F32),
                        pltpu.VMEM((HEAD_LANES, tq), F32)],
        compiler_params=_params("arbitrary", "arbitrary"),
        name=name,
    )(qT, k, vT)


def _pick_tile(n, candidates):
    for t in candidates:
        if n % t == 0:
            return t
    raise ValueError(f"no tile for {n}")


def _attention(qT, k, vT, group, L, Cn, name):
    T = L + Cn
    tq = _pick_tile(L, (512, 256, 128))
    tk = _pick_tile(T, (1280, 640, 256, 128))
    y_lat = _flash_call(qT, k, vT, group=group, q_rows=L, q_off=0, kv_rows=T, kv_off=0,
                        tq=tq, tk=tk, name=name)
    y_ctx = _flash_call(qT, k, vT, group=group, q_rows=Cn, q_off=L, kv_rows=Cn, kv_off=L,
                        tq=Cn, tk=Cn, name=name + "_ctx")
    return jnp.concatenate([y_lat, y_ctx], axis=0)


def _merge_kernel(x_ref, mod_ref, gmix_ref, yf_ref, yb_ref, rg_ref, ym_ref, yg_ref,
                  w_gates, w_ro, w_mo, w_go, w_out, o_ref):
    x = x_ref[...]
    mod = mod_ref[0]
    hb = _prenorm(x, mod, gmix_ref[...], 0, 1).astype(BF16)
    gs = jax.nn.sigmoid(_dot(hb, w_gates[...]))
    y = yf_ref[...] + yb_ref[...]
    g = _silu(rg_ref[...])
    parts = []
    for h in range(RET_HEADS):
        sl = slice(h * RET_DV, (h + 1) * RET_DV)
        parts.append((g[:, sl] * _rms(y[:, sl])).astype(BF16))
    y_ret = jnp.concatenate(parts, axis=-1)
    D = D_MODEL
    z = (gs[:, :D] * _dot(y_ret, w_ro[...])
         + gs[:, D:2 * D] * _dot(ym_ref[...], w_mo[...])
         + gs[:, 2 * D:] * _dot(yg_ref[...], w_go[...]))
    o_ref[...] = x + mod[2:3] * _dot(z.astype(BF16), w_out[...])


def _merge_call(xs, mod, gmix, yf, yb, rg, ym, yg, wts, n_lat_tiles):
    T, D = xs.shape
    tm = ROW_TILE
    row = lambda w: pl.BlockSpec((tm, w), lambda i: (i, 0))
    return pl.pallas_call(
        _merge_kernel,
        out_shape=jax.ShapeDtypeStruct((T, D), F32),
        grid=(T // tm,),
        in_specs=[row(D), pl.BlockSpec((1, MOD_ROWS, D), lambda i: (i // n_lat_tiles, 0, 0)),
                  _const_spec((1, D)), row(RET_W), row(RET_W), row(RET_W), row(MLA_W), row(GQA_W)]
                 + [_const_spec(w.shape) for w in wts],
        out_specs=row(D),
        compiler_params=_params("arbitrary"),
        name="merge",
    )(xs, mod, gmix, yf, yb, rg, ym, yg, *wts)


def _ffn_kernel(x_ref, mod_ref, g_ref, w_a, w_b, w_o, o_ref):
    x = x_ref[...]
    mod = mod_ref[0]
    hb = _prenorm(x, mod, g_ref[...], 3, 4).astype(BF16)
    u = (_silu(_dot(hb, w_a[...])) * _dot(hb, w_b[...])).astype(BF16)
    o_ref[...] = x + mod[5:6] * _dot(u, w_o[...])


def _ffn_call(xs, mod, g, w_a, w_b, w_o, n_lat_tiles):
    T, D = xs.shape
    tm = ROW_TILE
    row = pl.BlockSpec((tm, D), lambda i: (i, 0))
    return pl.pallas_call(
        _ffn_kernel,
        out_shape=jax.ShapeDtypeStruct((T, D), F32),
        grid=(T // tm,),
        in_specs=[row, pl.BlockSpec((1, MOD_ROWS, D), lambda i: (i // n_lat_tiles, 0, 0)),
                  _const_spec((1, D)), _const_spec(w_a.shape), _const_spec(w_b.shape),
                  _const_spec(w_o.shape)],
        out_specs=row,
        compiler_params=_params("arbitrary"),
        name="ffn",
    )(xs, mod, g, w_a, w_b, w_o)


def _final_norm_kernel(x_ref, g_ref, o_ref):
    o_ref[...] = _rms(x_ref[...]) * g_ref[...]


def _final_norm_call(xs, g, L):
    D = xs.shape[1]
    tm = ROW_TILE
    row = pl.BlockSpec((tm, D), lambda i: (i, 0))
    return pl.pallas_call(
        _final_norm_kernel,
        out_shape=jax.ShapeDtypeStruct((L, D), F32),
        grid=(L // tm,),
        in_specs=[row, _const_spec((1, D))],
        out_specs=row,
        compiler_params=_params("arbitrary"),
        name="final_norm",
    )(xs, g)


def _rope_tables(L, Cn):
    t = jnp.arange(L, dtype=jnp.int32)
    rows, cols = t // GRID_W, t % GRID_W

    def angles(pos, dim):
        half = dim // 2
        inv_freq = ROPE_BASE ** (-jnp.arange(half, dtype=F32) / half)
        ang = pos.astype(F32)[:, None] * inv_freq[None, :]
        return jnp.cos(ang), jnp.sin(ang)

    def tables(dim, lane0):
        cr, sr = angles(rows, dim // 2)
        cc, sc = angles(cols, dim // 2)
        z = jnp.zeros_like(sr)
        c = jnp.concatenate([cr, cr, cc, cc], axis=1)
        sa = jnp.concatenate([-sr, z, -sc, z], axis=1)
        sb = jnp.concatenate([z, sr, z, sc], axis=1)
        pad = lambda a, fill: jnp.concatenate(
            [jnp.full((L, lane0), fill, F32), a, jnp.full((L, HEAD_LANES - lane0 - dim), fill, F32)], axis=1)
        c, sa, sb = pad(c, 1.0), pad(sa, 0.0), pad(sb, 0.0)
        ctx = lambda fill: jnp.full((Cn, HEAD_LANES), fill, F32)
        return (jnp.concatenate([c, ctx(1.0)]), jnp.concatenate([sa, ctx(0.0)]),
                jnp.concatenate([sb, ctx(0.0)]))

    return tables(RET_DK, 0) + tables(MLA_ROPE, MLA_NOPE)


def _layer_weights(w_in, w_mla_qb, w_mla_kvb):
    split_at = np.cumsum(IN_SIZES)[:-1].tolist()
    rq, rk, rv, rg, cq, ckv, kr, gq, gk, gv, gates = jnp.split(w_in.astype(BF16), split_at, axis=-1)
    d = w_in.shape[0]
    kr_pad = jnp.pad(kr, ((0, 0), (MLA_NOPE, HEAD_LANES - MLA_NOPE - MLA_ROPE)))
    qb = w_mla_qb.astype(BF16).reshape(MLA_Q_RANK, MLA_HEADS, MLA_NOPE + MLA_ROPE)
    qb = jnp.pad(qb, ((0, 0), (0, 0), (0, HEAD_LANES - MLA_NOPE - MLA_ROPE))).reshape(MLA_Q_RANK, -1)
    kvb = w_mla_kvb.astype(BF16).reshape(MLA_KV_RANK, MLA_HEADS, MLA_NOPE + MLA_V)
    kb = jnp.pad(kvb[:, :, :MLA_NOPE], ((0, 0), (0, 0), (0, HEAD_LANES - MLA_NOPE))).reshape(MLA_KV_RANK, -1)
    vb = kvb[:, :, MLA_NOPE:].reshape(MLA_KV_RANK, -1)
    del d
    return (rq, rk, rv, rg, cq, ckv, kr_pad, gq, gk, gv), (qb, kb, vb), gates


def kernel(x, c, ctx, c_ctx, w_mod, b_mod, g_mix, w_in, g_mla_q, g_mla_kv, w_mla_qb, w_mla_kvb,
           g_gqa_q, g_gqa_k, w_ret_o, w_mla_o, w_gqa_o, w_out, g_ffn, w_ffn_in, w_ffn_out, g_final):
    B, L, D = x.shape
    Cn = ctx.shape[1]
    depth = w_mod.shape[0]
    assert B == 1 and D == D_MODEL and L % ROW_TILE == 0 and Cn % ROW_TILE == 0 and L % Cn == 0
    n_lat_tiles = L // ROW_TILE
    ffn_hidden = w_ffn_out.shape[1]

    xs = jnp.concatenate([x[0], ctx[0]], axis=0)
    cond = jnp.zeros((MOD_ROWS, D), F32).at[0].set(c[0]).at[1].set(c_ctx)
    mods = _mod_call(cond, w_mod, b_mod)
    mods = mods[:, :2].reshape(depth, 2, N_MOD, D)
    mods = jnp.pad(mods, ((0, 0), (0, 0), (0, MOD_ROWS - N_MOD), (0, 0)))

    tabs = _rope_tables(L, Cn)
    ret_consts = _ret_consts()
    row2 = lambda v: v.reshape(1, -1)

    for l in range(depth):
        in_w, mla_w, w_gates = _layer_weights(w_in[l], w_mla_qb[l], w_mla_kvb[l])
        wts = in_w + (row2(g_mla_q[l]), row2(g_mla_kv[l])) + mla_w + (row2(g_gqa_q[l]), row2(g_gqa_k[l]))
        rq, rk, rv, rg, mqT, mk, mvT, gqT, gk, gvT = _inproj_call(
            xs, mods[l], row2(g_mix[l]), tabs, wts, n_lat_tiles)
        yf, yb = _ret_call(rq, rk, rv, ret_consts, L // RET_CHUNK, Cn // RET_CHUNK)
        ym = _attention(mqT, mk, mvT, 1, L, Cn, "mla")
        yg = _attention(gqT, gk, gvT, GQA_HEADS // GQA_KV_HEADS, L, Cn, "gqa")
        merge_w = (w_gates, w_ret_o[l].astype(BF16), w_mla_o[l].astype(BF16),
                   w_gqa_o[l].astype(BF16), w_out[l].astype(BF16))
        xs = _merge_call(xs, mods[l], row2(g_mix[l]), yf, yb, rg, ym, yg, merge_w, n_lat_tiles)
        w_ffn = w_ffn_in[l].astype(BF16)
        xs = _ffn_call(xs, mods[l], row2(g_ffn[l]), w_ffn[:, :ffn_hidden], w_ffn[:, ffn_hidden:],
                       w_ffn_out[l].astype(BF16), n_lat_tiles)
    return _final_norm_call(xs, row2(g_final), L)[None]
```

```python
import functools
import math

import jax
import jax.numpy as jnp
import numpy as np
from jax import lax
from jax.experimental import pallas as pl
from jax.experimental.pallas import tpu as pltpu

D_MODEL = 1024
GRID_W = 64
RET_CHUNK = 128
ROPE_BASE = 10000.0
NORM_EPS = 1e-6
N_MOD = 6

RET_HEADS = 4
RET_DK = 128
RET_DV = 256
RET_DECAY_START = 5.0
RET_BWD_OFFSET = 0.5

MLA_HEADS = 8
MLA_Q_RANK = 256
MLA_KV_RANK = 256
MLA_NOPE = 64
MLA_ROPE = 32
MLA_V = 128

GQA_HEADS = 8
GQA_KV_HEADS = 2
GQA_HD = 128

RET_W = RET_HEADS * RET_DV
MLA_W = MLA_HEADS * MLA_V
GQA_W = GQA_HEADS * GQA_HD
N_BRANCH = 3

IN_SIZES = (
    RET_HEADS * RET_DK, RET_HEADS * RET_DK, RET_W, RET_W,
    MLA_Q_RANK, MLA_KV_RANK, MLA_ROPE,
    GQA_W, GQA_KV_HEADS * GQA_HD, GQA_KV_HEADS * GQA_HD,
    N_BRANCH * D_MODEL,
)

HEAD_LANES = 128
BF16_SUBLANES = 16
V_ROWS = HEAD_LANES + BF16_SUBLANES
MOD_ROWS = 8
ROW_TILE = 256
FLASH_UNROLL = 3
FLASH_SUB = 256
V7X_VMEM_LIMIT = 56 * 1024 * 1024
LOG2E = math.log2(math.e)
NEG_BIG = -1e30

F32 = jnp.float32
BF16 = jnp.bfloat16


def _params(*sem, flags=None):
    return pltpu.CompilerParams(dimension_semantics=sem, vmem_limit_bytes=V7X_VMEM_LIMIT, flags=flags)


def _const_spec(shape):
    zeros = (0,) * len(shape)
    return pl.BlockSpec(shape, lambda *_: zeros)


def _dot(a, b):
    return jnp.dot(a, b, preferred_element_type=F32)


def _rms(x):
    return x * lax.rsqrt(jnp.mean(x * x, axis=-1, keepdims=True) + NORM_EPS)


def _silu(x):
    return x * jax.nn.sigmoid(x)


def _rope(x, c, sa, sb, block):
    up = pltpu.roll(x, HEAD_LANES - block, 1)
    dn = pltpu.roll(x, block, 1)
    return x * c + up * sa + dn * sb


def _head(x, h):
    return x[:, h * HEAD_LANES:(h + 1) * HEAD_LANES]


def _mod_kernel(cond_ref, w_ref, b_ref, o_ref):
    s = _silu(cond_ref[...]).astype(BF16)
    o_ref[0] = _dot(s, w_ref[0].astype(BF16)) + b_ref[0]


def _mod_call(cond, w_mod, b_mod):
    depth, d, n = w_mod.shape
    tn = n // 4
    return pl.pallas_call(
        _mod_kernel,
        out_shape=jax.ShapeDtypeStruct((depth, MOD_ROWS, n), F32),
        grid=(depth, n // tn),
        in_specs=[
            pl.BlockSpec((MOD_ROWS, d), lambda l, j: (0, 0)),
            pl.BlockSpec((1, d, tn), lambda l, j: (l, 0, j)),
            pl.BlockSpec((1, 1, tn), lambda l, j: (l, 0, j)),
        ],
        out_specs=pl.BlockSpec((1, MOD_ROWS, tn), lambda l, j: (l, 0, j)),
        compiler_params=_params("arbitrary", "arbitrary"),
        name="mod",
    )(cond, w_mod, b_mod.reshape(depth, 1, n))


def _prenorm(x, mod, g, shift_row, scale_row):
    h = _rms(x) * g
    return h * (1.0 + mod[scale_row:scale_row + 1]) + mod[shift_row:shift_row + 1]


def _inproj_kernel(x_ref, mod_ref, gmix_ref, c_ref, sa_ref, sb_ref, cm_ref, sam_ref, sbm_ref,
                   w_rq, w_rk, w_rv, w_rg, w_cq, w_ckv, w_kr, w_gq, w_gk, w_gv,
                   g_mq, g_mkv, w_qb, w_kb, w_vb, g_gq, g_gk,
                   rq_o, rk_o, rv_o, rg_o, mqT_o, mk_o, mvT_o, gqT_o, gk_o, gvT_o):
    hb = _prenorm(x_ref[...], mod_ref[0], gmix_ref[...], 0, 1).astype(BF16)
    c, sa, sb = c_ref[...], sa_ref[...], sb_ref[...]
    cm, sam, sbm = cm_ref[...], sam_ref[...], sbm_ref[...]
    rope_big = lambda v: _rope(v, c, sa, sb, RET_DK // 4)
    rope_mla = lambda v: _rope(v, cm, sam, sbm, MLA_ROPE // 4)
    ones_rows = (lax.broadcasted_iota(jnp.int32, (BF16_SUBLANES, x_ref.shape[0]), 0) == 0).astype(BF16)

    rq = _dot(hb, w_rq[...])
    rk = _dot(hb, w_rk[...])
    for h in range(RET_HEADS):
        sl = slice(h * HEAD_LANES, (h + 1) * HEAD_LANES)
        rq_o[:, sl] = rope_big(_head(rq, h))
        rk_o[:, sl] = rope_big(_head(rk, h)) * (RET_DK ** -0.5)
    rv_o[...] = _dot(hb, w_rv[...]).astype(BF16)
    rg_o[...] = _dot(hb, w_rg[...])

    mla_qscale = (MLA_NOPE + MLA_ROPE) ** -0.5 * LOG2E
    cqn = (_rms(_dot(hb, w_cq[...])) * g_mq[...]).astype(BF16)
    q = _dot(cqn, w_qb[...])
    for h in range(MLA_HEADS):
        mqT_o[h] = (rope_mla(_head(q, h)) * mla_qscale).T.astype(BF16)
    ckvn = (_rms(_dot(hb, w_ckv[...])) * g_mkv[...]).astype(BF16)
    kn = _dot(ckvn, w_kb[...])
    kr = rope_mla(_dot(hb, w_kr[...]))
    v = _dot(ckvn, w_vb[...])
    for h in range(MLA_HEADS):
        sl = slice(h * HEAD_LANES, (h + 1) * HEAD_LANES)
        mk_o[:, sl] = (_head(kn, h) + kr).astype(BF16)
        mvT_o[h, :HEAD_LANES, :] = _head(v, h).T.astype(BF16)
        mvT_o[h, HEAD_LANES:, :] = ones_rows

    gqa_qscale = GQA_HD ** -0.5 * LOG2E
    gq = _dot(hb, w_gq[...])
    for h in range(GQA_HEADS):
        qh = rope_big(_rms(_head(gq, h)) * g_gq[...]) * gqa_qscale
        gqT_o[h] = qh.T.astype(BF16)
    gk = _dot(hb, w_gk[...])
    gv = _dot(hb, w_gv[...])
    for h in range(GQA_KV_HEADS):
        sl = slice(h * HEAD_LANES, (h + 1) * HEAD_LANES)
        gk_o[:, sl] = rope_big(_rms(_head(gk, h)) * g_gk[...]).astype(BF16)
        gvT_o[h, :HEAD_LANES, :] = _head(gv, h).T.astype(BF16)
        gvT_o[h, HEAD_LANES:, :] = ones_rows


def _inproj_call(xs, mod, gmix, tabs, wts, n_lat_tiles):
    T, D = xs.shape
    tm = ROW_TILE
    row = lambda w: pl.BlockSpec((tm, w), lambda i: (i, 0))
    headsT = lambda nh, rows=HEAD_LANES: pl.BlockSpec((nh, rows, tm), lambda i: (0, 0, i))
    in_specs = [
        row(D),
        pl.BlockSpec((1, MOD_ROWS, D), lambda i: (i // n_lat_tiles, 0, 0)),
        _const_spec((1, D)),
    ] + [row(HEAD_LANES)] * 6 + [_const_spec(w.shape) for w in wts]
    out_shape = (
        jax.ShapeDtypeStruct((T, RET_HEADS * RET_DK), F32),
        jax.ShapeDtypeStruct((T, RET_HEADS * RET_DK), F32),
        jax.ShapeDtypeStruct((T, RET_W), BF16),
        jax.ShapeDtypeStruct((T, RET_W), F32),
        jax.ShapeDtypeStruct((MLA_HEADS, HEAD_LANES, T), BF16),
        jax.ShapeDtypeStruct((T, MLA_HEADS * HEAD_LANES), BF16),
        jax.ShapeDtypeStruct((MLA_HEADS, V_ROWS, T), BF16),
        jax.ShapeDtypeStruct((GQA_HEADS, HEAD_LANES, T), BF16),
        jax.ShapeDtypeStruct((T, GQA_KV_HEADS * GQA_HD), BF16),
        jax.ShapeDtypeStruct((GQA_KV_HEADS, V_ROWS, T), BF16),
    )
    out_specs = (
        row(RET_HEADS * RET_DK), row(RET_HEADS * RET_DK), row(RET_W), row(RET_W),
        headsT(MLA_HEADS), row(MLA_HEADS * HEAD_LANES), headsT(MLA_HEADS, V_ROWS),
        headsT(GQA_HEADS), row(GQA_KV_HEADS * GQA_HD), headsT(GQA_KV_HEADS, V_ROWS),
    )
    return pl.pallas_call(
        _inproj_kernel,
        out_shape=out_shape,
        grid=(T // tm,),
        in_specs=in_specs,
        out_specs=out_specs,
        compiler_params=_params("arbitrary"),
        name="inproj",
    )(xs, mod, gmix, *tabs, *wts)


def _ret_kernel(qf_ref, kf_ref, vf_ref, qb_ref, kb_ref, vb_ref,
                dec_ref, xi_ref, zeta_ref, cd_ref,
                yf_ref, yb_ref, sf_ref, sb_ref):
    @pl.when(pl.program_id(0) == 0)
    def _():
        sf_ref[...] = jnp.zeros_like(sf_ref)
        sb_ref[...] = jnp.zeros_like(sb_ref)

    for d, (q_ref, k_ref, v_ref, y_ref, s_ref) in enumerate(
            ((qf_ref, kf_ref, vf_ref, yf_ref, sf_ref), (qb_ref, kb_ref, vb_ref, yb_ref, sb_ref))):
        for h in range(RET_HEADS):
            ksl = slice(h * RET_DK, (h + 1) * RET_DK)
            vsl = slice(h * RET_DV, (h + 1) * RET_DV)
            q = q_ref[:, ksl]
            k = k_ref[:, ksl]
            v = v_ref[:, vsl]
            qb16 = q.astype(BF16)
            scores = lax.dot_general(qb16, k.astype(BF16), (((1,), (1,)), ((), ())),
                                     preferred_element_type=F32) * dec_ref[d, h]
            inner = _dot(scores.astype(BF16), v)
            state = s_ref[h]
            cross = _dot(qb16, state.astype(BF16)) * xi_ref[d, h]
            y_ref[:, vsl] = inner + cross
            kz = (k * zeta_ref[d, h]).T.astype(BF16)
            s_ref[h] = cd_ref[d, h] * state + _dot(kz, v)


def _ret_call(rq, rk, rv, consts, n_lat, n_ctx):
    T = rq.shape[0]
    C = RET_CHUNK
    n = n_lat + n_ctx

    def fwd(i):
        return jnp.where(i < n_ctx, n_lat + i, i - n_ctx)

    def bwd(i):
        return jnp.where(i < n_ctx, n - 1 - i, n - 1 - i)

    kw, vw = RET_HEADS * RET_DK, RET_W
    spec = lambda w, order: pl.BlockSpec((C, w), lambda i: (order(i), 0))
    dec, xi, zeta, cd = consts
    return pl.pallas_call(
        _ret_kernel,
        out_shape=(jax.ShapeDtypeStruct((T, vw), F32), jax.ShapeDtypeStruct((T, vw), F32)),
        grid=(n,),
        in_specs=[spec(kw, fwd), spec(kw, fwd), spec(vw, fwd),
                  spec(kw, bwd), spec(kw, bwd), spec(vw, bwd),
                  _const_spec(dec.shape), _const_spec(xi.shape),
                  _const_spec(zeta.shape), _const_spec(cd.shape)],
        out_specs=(spec(vw, fwd), spec(vw, bwd)),
        scratch_shapes=[pltpu.VMEM((RET_HEADS, RET_DK, RET_DV), F32),
                        pltpu.VMEM((RET_HEADS, RET_DK, RET_DV), F32)],
        compiler_params=_params("arbitrary"),
        name="retention",
    )(rq, rk, rv, rq, rk, rv, dec, xi, zeta, cd)


def _ret_consts():
    C = RET_CHUNK
    h = jnp.arange(RET_HEADS, dtype=F32)
    pos = jnp.arange(C, dtype=F32)
    diff = pos[:, None] - pos[None, :]

    def one(offset, reverse):
        lg = jnp.log1p(-jnp.exp2(-(RET_DECAY_START + offset) - h))
        dd = -diff if reverse else diff
        keep = dd >= 0
        dec = jnp.where(keep[None], jnp.exp(lg[:, None, None] * jnp.where(keep, dd, 0.0)[None]), 0.0)
        p = (C - 1 - pos) if reverse else pos
        xi = jnp.exp(lg[:, None] * (p + 1)[None, :])
        zeta = jnp.exp(lg[:, None] * (C - 1 - p)[None, :])
        cd = jnp.exp(lg * C)
        return dec, xi[:, :, None], zeta[:, :, None], jnp.broadcast_to(cd[:, None, None], (RET_HEADS, 1, 1))

    f = one(0.0, False)
    b = one(RET_BWD_OFFSET, True)
    return tuple(jnp.stack([a, c]) for a, c in zip(f, b))


def _flash_kernel(qT_ref, k_ref, vT_ref, o_ref, m_ref, acc_ref,
                  s0_ref, s1_ref, s2_ref, c0_ref, c1_ref, c2_ref, *, tk, nk, unroll, sub):
    q = qT_ref[0]
    m_ref[...] = jnp.full_like(m_ref, NEG_BIG)
    acc_ref[...] = jnp.zeros_like(acc_ref)
    bufs = ((s0_ref, c0_ref), (s1_ref, c1_ref), (s2_ref, c2_ref))
    n_sub = tk // sub

    def key_rows(j, r):
        base = j * tk if isinstance(j, int) else pl.multiple_of(j * tk, tk)
        return pl.ds(base + r * sub, sub)

    def step(j, phase, do_scores, do_values):
        s_cur, cmax_cur = bufs[phase]
        s_nxt, cmax_nxt = bufs[(phase + 2) % 3]
        if do_values:
            m_prev = m_ref[...]
            m_new = jnp.maximum(m_prev, cmax_cur[...])
            alpha = jnp.exp2(m_prev - m_new)
            m_ref[...] = m_new
        cmax = pv = None
        for r in range(n_sub):
            rows = pl.ds(r * sub, sub)
            if do_values:
                p = jnp.exp2(s_cur[rows, :] - m_new).astype(BF16)
                part = _dot(vT_ref[0, :, key_rows(j, r)], p)
                pv = part if pv is None else pv + part
            if do_scores:
                s = _dot(k_ref[key_rows(j + 2, r), :], q)
                s_nxt[rows, :] = s
                part = jnp.max(s, axis=0, keepdims=True)
                cmax = part if cmax is None else jnp.maximum(cmax, part)
        if do_scores:
            cmax_nxt[...] = cmax
        if do_values:
            acc_ref[...] = alpha * acc_ref[...] + pv

    def body(i, carry):
        for u in range(unroll):
            step(i * unroll + u, u % 3, True, True)
        return carry

    for j in range(-2, 0):
        step(j, j % 3, j + 2 < nk, False)
    n_body = max(nk - 2, 0) // unroll
    if n_body < 2:
        n_body = 0
    if n_body:
        lax.fori_loop(0, n_body, body, 0)
    for j in range(n_body * unroll, nk):
        step(j, j % 3, j + 2 < nk, True)
    acc = acc_ref[...]
    o_ref[...] = (acc[:HEAD_LANES] / acc[HEAD_LANES:HEAD_LANES + 1]).T.astype(o_ref.dtype)


def _flash_call(qT, k, vT, *, group, q_rows, q_off, kv_rows, kv_off, tq, tk, name):
    hq = qT.shape[0]
    sub = min(tk, FLASH_SUB)
    assert q_rows % tq == 0 and kv_rows % tk == 0 and tk % sub == 0 and FLASH_UNROLL % 3 == 0
    nq, nk = q_rows // tq, kv_rows // tk
    qb, kb = q_off // tq, kv_off // kv_rows
    return pl.pallas_call(
        functools.partial(_flash_kernel, tk=tk, nk=nk, unroll=FLASH_UNROLL, sub=sub),
        out_shape=jax.ShapeDtypeStruct((q_rows, hq * HEAD_LANES), BF16),
        grid=(hq, nq),
        in_specs=[
            pl.BlockSpec((1, HEAD_LANES, tq), lambda h, i: (h, 0, i + qb)),
            pl.BlockSpec((kv_rows, HEAD_LANES), lambda h, i: (kb, h // group)),
            pl.BlockSpec((1, V_ROWS, kv_rows), lambda h, i: (h // group, 0, kb)),
        ],
        out_specs=pl.BlockSpec((tq, HEAD_LANES), lambda h, i: (i, h)),
        scratch_shapes=[pltpu.VMEM((1, tq), F32), pltpu.VMEM((V_ROWS, tq), F32),
                        pltpu.VMEM((tk, tq), F32), pltpu.VMEM((tk, tq), F32), pltpu.VMEM((tk, tq), F32),
                        pltpu.VMEM((1, tq), F32), pltpu.VMEM((1, tq), F32), pltpu.VMEM((1, tq), F32)],
        compiler_params=_params("arbitrary", "arbitrary"),
        name=name,
    )(qT, k, vT)


def _pick_tile(n, candidates):
    for t in candidates:
        if n % t == 0:
            return t
    raise ValueError(f"no tile for {n}")


def _attention(qT, k, vT, group, L, Cn, name):
    T = L + Cn
    tq = _pick_tile(L, (512, 256, 128))
    tk = _pick_tile(T, (1280, 256, 128))
    y_lat = _flash_call(qT, k, vT, group=group, q_rows=L, q_off=0, kv_rows=T, kv_off=0,
                        tq=tq, tk=tk, name=name)
    y_ctx = _flash_call(qT, k, vT, group=group, q_rows=Cn, q_off=L, kv_rows=Cn, kv_off=L,
                        tq=Cn, tk=Cn, name=name + "_ctx")
    return jnp.concatenate([y_lat, y_ctx], axis=0)


def _merge_kernel(x_ref, mod_ref, gmix_ref, yf_ref, yb_ref, rg_ref, ym_ref, yg_ref,
                  w_gates, w_ro, w_mo, w_go, w_out, o_ref):
    x = x_ref[...]
    mod = mod_ref[0]
    hb = _prenorm(x, mod, gmix_ref[...], 0, 1).astype(BF16)
    gs = jax.nn.sigmoid(_dot(hb, w_gates[...]))
    y = yf_ref[...] + yb_ref[...]
    g = _silu(rg_ref[...])
    parts = []
    for h in range(RET_HEADS):
        sl = slice(h * RET_DV, (h + 1) * RET_DV)
        parts.append((g[:, sl] * _rms(y[:, sl])).astype(BF16))
    y_ret = jnp.concatenate(parts, axis=-1)
    D = D_MODEL
    z = (gs[:, :D] * _dot(y_ret, w_ro[...])
         + gs[:, D:2 * D] * _dot(ym_ref[...], w_mo[...])
         + gs[:, 2 * D:] * _dot(yg_ref[...], w_go[...]))
    o_ref[...] = x + mod[2:3] * _dot(z.astype(BF16), w_out[...])


def _merge_call(xs, mod, gmix, yf, yb, rg, ym, yg, wts, n_lat_tiles):
    T, D = xs.shape
    tm = ROW_TILE
    row = lambda w: pl.BlockSpec((tm, w), lambda i: (i, 0))
    return pl.pallas_call(
        _merge_kernel,
        out_shape=jax.ShapeDtypeStruct((T, D), F32),
        grid=(T // tm,),
        in_specs=[row(D), pl.BlockSpec((1, MOD_ROWS, D), lambda i: (i // n_lat_tiles, 0, 0)),
                  _const_spec((1, D)), row(RET_W), row(RET_W), row(RET_W), row(MLA_W), row(GQA_W)]
                 + [_const_spec(w.shape) for w in wts],
        out_specs=row(D),
        compiler_params=_params("arbitrary"),
        name="merge",
    )(xs, mod, gmix, yf, yb, rg, ym, yg, *wts)


def _ffn_kernel(x_ref, mod_ref, g_ref, w_a, w_b, w_o, o_ref):
    x = x_ref[...]
    mod = mod_ref[0]
    hb = _prenorm(x, mod, g_ref[...], 3, 4).astype(BF16)
    u = (_silu(_dot(hb, w_a[...])) * _dot(hb, w_b[...])).astype(BF16)
    o_ref[...] = x + mod[5:6] * _dot(u, w_o[...])


def _ffn_call(xs, mod, g, w_a, w_b, w_o, n_lat_tiles):
    T, D = xs.shape
    tm = ROW_TILE
    row = pl.BlockSpec((tm, D), lambda i: (i, 0))
    return pl.pallas_call(
        _ffn_kernel,
        out_shape=jax.ShapeDtypeStruct((T, D), F32),
        grid=(T // tm,),
        in_specs=[row, pl.BlockSpec((1, MOD_ROWS, D), lambda i: (i // n_lat_tiles, 0, 0)),
                  _const_spec((1, D)), _const_spec(w_a.shape), _const_spec(w_b.shape),
                  _const_spec(w_o.shape)],
        out_specs=row,
        compiler_params=_params("arbitrary"),
        name="ffn",
    )(xs, mod, g, w_a, w_b, w_o)


def _final_norm_kernel(x_ref, g_ref, o_ref):
    o_ref[...] = _rms(x_ref[...]) * g_ref[...]


def _final_norm_call(xs, g, L):
    D = xs.shape[1]
    tm = ROW_TILE
    row = pl.BlockSpec((tm, D), lambda i: (i, 0))
    return pl.pallas_call(
        _final_norm_kernel,
        out_shape=jax.ShapeDtypeStruct((L, D), F32),
        grid=(L // tm,),
        in_specs=[row, _const_spec((1, D))],
        out_specs=row,
        compiler_params=_params("arbitrary"),
        name="final_norm",
    )(xs, g)


def _rope_tables(L, Cn):
    t = jnp.arange(L, dtype=jnp.int32)
    rows, cols = t // GRID_W, t % GRID_W

    def angles(pos, dim):
        half = dim // 2
        inv_freq = ROPE_BASE ** (-jnp.arange(half, dtype=F32) / half)
        ang = pos.astype(F32)[:, None] * inv_freq[None, :]
        return jnp.cos(ang), jnp.sin(ang)

    def tables(dim, lane0):
        cr, sr = angles(rows, dim // 2)
        cc, sc = angles(cols, dim // 2)
        z = jnp.zeros_like(sr)
        c = jnp.concatenate([cr, cr, cc, cc], axis=1)
        sa = jnp.concatenate([-sr, z, -sc, z], axis=1)
        sb = jnp.concatenate([z, sr, z, sc], axis=1)
        pad = lambda a, fill: jnp.concatenate(
            [jnp.full((L, lane0), fill, F32), a, jnp.full((L, HEAD_LANES - lane0 - dim), fill, F32)], axis=1)
        c, sa, sb = pad(c, 1.0), pad(sa, 0.0), pad(sb, 0.0)
        ctx = lambda fill: jnp.full((Cn, HEAD_LANES), fill, F32)
        return (jnp.concatenate([c, ctx(1.0)]), jnp.concatenate([sa, ctx(0.0)]),
                jnp.concatenate([sb, ctx(0.0)]))

    return tables(RET_DK, 0) + tables(MLA_ROPE, MLA_NOPE)


def _layer_weights(w_in, w_mla_qb, w_mla_kvb):
    split_at = np.cumsum(IN_SIZES)[:-1].tolist()
    rq, rk, rv, rg, cq, ckv, kr, gq, gk, gv, gates = jnp.split(w_in.astype(BF16), split_at, axis=-1)
    d = w_in.shape[0]
    kr_pad = jnp.pad(kr, ((0, 0), (MLA_NOPE, HEAD_LANES - MLA_NOPE - MLA_ROPE)))
    qb = w_mla_qb.astype(BF16).reshape(MLA_Q_RANK, MLA_HEADS, MLA_NOPE + MLA_ROPE)
    qb = jnp.pad(qb, ((0, 0), (0, 0), (0, HEAD_LANES - MLA_NOPE - MLA_ROPE))).reshape(MLA_Q_RANK, -1)
    kvb = w_mla_kvb.astype(BF16).reshape(MLA_KV_RANK, MLA_HEADS, MLA_NOPE + MLA_V)
    kb = jnp.pad(kvb[:, :, :MLA_NOPE], ((0, 0), (0, 0), (0, HEAD_LANES - MLA_NOPE))).reshape(MLA_KV_RANK, -1)
    vb = kvb[:, :, MLA_NOPE:].reshape(MLA_KV_RANK, -1)
    del d
    return (rq, rk, rv, rg, cq, ckv, kr_pad, gq, gk, gv), (qb, kb, vb), gates


def kernel(x, c, ctx, c_ctx, w_mod, b_mod, g_mix, w_in, g_mla_q, g_mla_kv, w_mla_qb, w_mla_kvb,
           g_gqa_q, g_gqa_k, w_ret_o, w_mla_o, w_gqa_o, w_out, g_ffn, w_ffn_in, w_ffn_out, g_final):
    B, L, D = x.shape
    Cn = ctx.shape[1]
    depth = w_mod.shape[0]
    assert B == 1 and D == D_MODEL and L % ROW_TILE == 0 and Cn % ROW_TILE == 0 and L % Cn == 0
    n_lat_tiles = L // ROW_TILE
    ffn_hidden = w_ffn_out.shape[1]

    xs = jnp.concatenate([x[0], ctx[0]], axis=0)
    cond = jnp.zeros((MOD_ROWS, D), F32).at[0].set(c[0]).at[1].set(c_ctx)
    mods = _mod_call(cond, w_mod, b_mod)
    mods = mods[:, :2].reshape(depth, 2, N_MOD, D)
    mods = jnp.pad(mods, ((0, 0), (0, 0), (0, MOD_ROWS - N_MOD), (0, 0)))

    tabs = _rope_tables(L, Cn)
    ret_consts = _ret_consts()
    row2 = lambda v: v.reshape(1, -1)

    for l in range(depth):
        in_w, mla_w, w_gates = _layer_weights(w_in[l], w_mla_qb[l], w_mla_kvb[l])
        wts = in_w + (row2(g_mla_q[l]), row2(g_mla_kv[l])) + mla_w + (row2(g_gqa_q[l]), row2(g_gqa_k[l]))
        rq, rk, rv, rg, mqT, mk, mvT, gqT, gk, gvT = _inproj_call(
            xs, mods[l], row2(g_mix[l]), tabs, wts, n_lat_tiles)
        yf, yb = _ret_call(rq, rk, rv, ret_consts, L // RET_CHUNK, Cn // RET_CHUNK)
        ym = _attention(mqT, mk, mvT, 1, L, Cn, "mla")
        yg = _attention(gqT, gk, gvT, GQA_HEADS // GQA_KV_HEADS, L, Cn, "gqa")
        merge_w = (w_gates, w_ret_o[l].astype(BF16), w_mla_o[l].astype(BF16),
                   w_gqa_o[l].astype(BF16), w_out[l].astype(BF16))
        xs = _merge_call(xs, mods[l], row2(g_mix[l]), yf, yb, rg, ym, yg, merge_w, n_lat_tiles)
        w_ffn = w_ffn_in[l].astype(BF16)
        xs = _ffn_call(xs, mods[l], row2(g_ffn[l]), w_ffn[:, :ffn_hidden], w_ffn[:, ffn_hidden:],
                       w_ffn_out[l].astype(BF16), n_lat_tiles)
    return _final_norm_call(xs, row2(g_final), L)[None]
```

```python
import functools
import math

import jax
import jax.numpy as jnp
import numpy as np
from jax import lax
from jax.experimental import pallas as pl
from jax.experimental.pallas import tpu as pltpu

D_MODEL = 1024
GRID_W = 64
RET_CHUNK = 128
ROPE_BASE = 10000.0
NORM_EPS = 1e-6
N_MOD = 6

RET_HEADS = 4
RET_DK = 128
RET_DV = 256
RET_DECAY_START = 5.0
RET_BWD_OFFSET = 0.5

MLA_HEADS = 8
MLA_Q_RANK = 256
MLA_KV_RANK = 256
MLA_NOPE = 64
MLA_ROPE = 32
MLA_V = 128

GQA_HEADS = 8
GQA_KV_HEADS = 2
GQA_HD = 128

RET_W = RET_HEADS * RET_DV
MLA_W = MLA_HEADS * MLA_V
GQA_W = GQA_HEADS * GQA_HD
N_BRANCH = 3

IN_SIZES = (
    RET_HEADS * RET_DK, RET_HEADS * RET_DK, RET_W, RET_W,
    MLA_Q_RANK, MLA_KV_RANK, MLA_ROPE,
    GQA_W, GQA_KV_HEADS * GQA_HD, GQA_KV_HEADS * GQA_HD,
    N_BRANCH * D_MODEL,
)

HEAD_LANES = 128
BF16_SUBLANES = 16
V_ROWS = HEAD_LANES + BF16_SUBLANES
MOD_ROWS = 8
ROW_TILE = 256
FLASH_LAG = 2
FLASH_UNROLL = 3
FLASH_SUB = 256
V7X_VMEM_LIMIT = 56 * 1024 * 1024
LOG2E = math.log2(math.e)
NEG_BIG = -1e30

F32 = jnp.float32
BF16 = jnp.bfloat16


def _params(*sem, flags=None):
    return pltpu.CompilerParams(dimension_semantics=sem, vmem_limit_bytes=V7X_VMEM_LIMIT, flags=flags)


def _const_spec(shape):
    zeros = (0,) * len(shape)
    return pl.BlockSpec(shape, lambda *_: zeros)


def _layer_spec(w, l):
    zeros = (0,) * (w.ndim - 1)
    return pl.BlockSpec((None,) + w.shape[1:], lambda *_: (l,) + zeros)


def _mod_spec(l, n_lat_tiles):
    return pl.BlockSpec((None, 1, MOD_ROWS, D_MODEL), lambda i: (l, i // n_lat_tiles, 0, 0))


def _dot(a, b):
    return jnp.dot(a, b, preferred_element_type=F32)


def _rms(x):
    return x * lax.rsqrt(jnp.mean(x * x, axis=-1, keepdims=True) + NORM_EPS)


def _silu(x):
    return x * jax.nn.sigmoid(x)


def _rope(x, c, sa, sb, block):
    up = pltpu.roll(x, HEAD_LANES - block, 1)
    dn = pltpu.roll(x, block, 1)
    return x * c + up * sa + dn * sb


def _head(x, h):
    return x[:, h * HEAD_LANES:(h + 1) * HEAD_LANES]


def _mod_kernel(cond_ref, w_ref, b_ref, o_ref):
    s = _silu(cond_ref[...]).astype(BF16)
    o_ref[0] = _dot(s, w_ref[0].astype(BF16)) + b_ref[0]


def _mod_call(cond, w_mod, b_mod):
    depth, d, n = w_mod.shape
    tn = n // 4
    return pl.pallas_call(
        _mod_kernel,
        out_shape=jax.ShapeDtypeStruct((depth, MOD_ROWS, n), F32),
        grid=(depth, n // tn),
        in_specs=[
            pl.BlockSpec((MOD_ROWS, d), lambda l, j: (0, 0)),
            pl.BlockSpec((1, d, tn), lambda l, j: (l, 0, j)),
            pl.BlockSpec((1, 1, tn), lambda l, j: (l, 0, j)),
        ],
        out_specs=pl.BlockSpec((1, MOD_ROWS, tn), lambda l, j: (l, 0, j)),
        compiler_params=_params("arbitrary", "arbitrary"),
        name="mod",
    )(cond, w_mod, b_mod.reshape(depth, 1, n))


def _prenorm(x, mod, g, shift_row, scale_row):
    h = _rms(x) * g
    return h * (1.0 + mod[scale_row:scale_row + 1]) + mod[shift_row:shift_row + 1]


def _inproj_kernel(x_ref, mod_ref, gmix_ref, c_ref, sa_ref, sb_ref, cm_ref, sam_ref, sbm_ref,
                   w_rq, w_rk, w_rv, w_rg, w_cq, w_ckv, w_kr, w_gq, w_gk, w_gv,
                   g_mq, g_mkv, w_qb, w_kb, w_vb, g_gq, g_gk,
                   rq_o, rk_o, rv_o, rg_o, mqT_o, mk_o, mvT_o, gqT_o, gk_o, gvT_o):
    hb = _prenorm(x_ref[...], mod_ref[0], gmix_ref[...], 0, 1).astype(BF16)
    c, sa, sb = c_ref[...], sa_ref[...], sb_ref[...]
    cm, sam, sbm = cm_ref[...], sam_ref[...], sbm_ref[...]
    rope_big = lambda v: _rope(v, c, sa, sb, RET_DK // 4)
    rope_mla = lambda v: _rope(v, cm, sam, sbm, MLA_ROPE // 4)
    ones_rows = (lax.broadcasted_iota(jnp.int32, (BF16_SUBLANES, x_ref.shape[0]), 0) == 0).astype(BF16)

    rq = _dot(hb, w_rq[...])
    rk = _dot(hb, w_rk[...])
    for h in range(RET_HEADS):
        sl = slice(h * HEAD_LANES, (h + 1) * HEAD_LANES)
        rq_o[:, sl] = rope_big(_head(rq, h))
        rk_o[:, sl] = rope_big(_head(rk, h)) * (RET_DK ** -0.5)
    rv_o[...] = _dot(hb, w_rv[...]).astype(BF16)
    rg_o[...] = _dot(hb, w_rg[...])

    mla_qscale = (MLA_NOPE + MLA_ROPE) ** -0.5 * LOG2E
    cqn = (_rms(_dot(hb, w_cq[...])) * g_mq[...]).astype(BF16)
    q = _dot(cqn, w_qb[...])
    for h in range(MLA_HEADS):
        mqT_o[h] = (rope_mla(_head(q, h)) * mla_qscale).T.astype(BF16)
    ckvn = (_rms(_dot(hb, w_ckv[...])) * g_mkv[...]).astype(BF16)
    kn = _dot(ckvn, w_kb[...])
    kr = rope_mla(_dot(hb, w_kr[...]))
    v = _dot(ckvn, w_vb[...])
    for h in range(MLA_HEADS):
        sl = slice(h * HEAD_LANES, (h + 1) * HEAD_LANES)
        mk_o[:, sl] = (_head(kn, h) + kr).astype(BF16)
        mvT_o[h, :HEAD_LANES, :] = _head(v, h).T.astype(BF16)
        mvT_o[h, HEAD_LANES:, :] = ones_rows

    gqa_qscale = GQA_HD ** -0.5 * LOG2E
    gq = _dot(hb, w_gq[...])
    for h in range(GQA_HEADS):
        qh = rope_big(_rms(_head(gq, h)) * g_gq[...]) * gqa_qscale
        gqT_o[h] = qh.T.astype(BF16)
    gk = _dot(hb, w_gk[...])
    gv = _dot(hb, w_gv[...])
    for h in range(GQA_KV_HEADS):
        sl = slice(h * HEAD_LANES, (h + 1) * HEAD_LANES)
        gk_o[:, sl] = rope_big(_rms(_head(gk, h)) * g_gk[...]).astype(BF16)
        gvT_o[h, :HEAD_LANES, :] = _head(gv, h).T.astype(BF16)
        gvT_o[h, HEAD_LANES:, :] = ones_rows


def _inproj_call(xs, mods, gmix, tabs, wts, n_lat_tiles, l):
    T, D = xs.shape
    tm = ROW_TILE
    row = lambda w: pl.BlockSpec((tm, w), lambda i: (i, 0))
    headsT = lambda nh, rows=HEAD_LANES: pl.BlockSpec((nh, rows, tm), lambda i: (0, 0, i))
    in_specs = [row(D), _mod_spec(l, n_lat_tiles), _layer_spec(gmix, l)]
    in_specs += [row(HEAD_LANES)] * 6 + [_layer_spec(w, l) for w in wts]
    out_shape = (
        jax.ShapeDtypeStruct((T, RET_HEADS * RET_DK), F32),
        jax.ShapeDtypeStruct((T, RET_HEADS * RET_DK), F32),
        jax.ShapeDtypeStruct((T, RET_W), BF16),
        jax.ShapeDtypeStruct((T, RET_W), F32),
        jax.ShapeDtypeStruct((MLA_HEADS, HEAD_LANES, T), BF16),
        jax.ShapeDtypeStruct((T, MLA_HEADS * HEAD_LANES), BF16),
        jax.ShapeDtypeStruct((MLA_HEADS, V_ROWS, T), BF16),
        jax.ShapeDtypeStruct((GQA_HEADS, HEAD_LANES, T), BF16),
        jax.ShapeDtypeStruct((T, GQA_KV_HEADS * GQA_HD), BF16),
        jax.ShapeDtypeStruct((GQA_KV_HEADS, V_ROWS, T), BF16),
    )
    out_specs = (
        row(RET_HEADS * RET_DK), row(RET_HEADS * RET_DK), row(RET_W), row(RET_W),
        headsT(MLA_HEADS), row(MLA_HEADS * HEAD_LANES), headsT(MLA_HEADS, V_ROWS),
        headsT(GQA_HEADS), row(GQA_KV_HEADS * GQA_HD), headsT(GQA_KV_HEADS, V_ROWS),
    )
    return pl.pallas_call(
        _inproj_kernel,
        out_shape=out_shape,
        grid=(T // tm,),
        in_specs=in_specs,
        out_specs=out_specs,
        compiler_params=_params("arbitrary"),
        name="inproj",
    )(xs, mods, gmix, *tabs, *wts)


def _ret_kernel(qf_ref, kf_ref, vf_ref, qb_ref, kb_ref, vb_ref,
                dec_ref, xi_ref, zeta_ref, cd_ref,
                yf_ref, yb_ref, sf_ref, sb_ref):
    @pl.when(pl.program_id(0) == 0)
    def _():
        sf_ref[...] = jnp.zeros_like(sf_ref)
        sb_ref[...] = jnp.zeros_like(sb_ref)

    for d, (q_ref, k_ref, v_ref, y_ref, s_ref) in enumerate(
            ((qf_ref, kf_ref, vf_ref, yf_ref, sf_ref), (qb_ref, kb_ref, vb_ref, yb_ref, sb_ref))):
        for h in range(RET_HEADS):
            ksl = slice(h * RET_DK, (h + 1) * RET_DK)
            vsl = slice(h * RET_DV, (h + 1) * RET_DV)
            q = q_ref[:, ksl]
            k = k_ref[:, ksl]
            v = v_ref[:, vsl]
            qb16 = q.astype(BF16)
            scores = lax.dot_general(qb16, k.astype(BF16), (((1,), (1,)), ((), ())),
                                     preferred_element_type=F32) * dec_ref[d, h]
            inner = _dot(scores.astype(BF16), v)
            state = s_ref[h]
            cross = _dot(qb16, state.astype(BF16)) * xi_ref[d, h]
            y_ref[:, vsl] = inner + cross
            kz = (k * zeta_ref[d, h]).T.astype(BF16)
            s_ref[h] = cd_ref[d, h] * state + _dot(kz, v)


def _ret_call(rq, rk, rv, consts, n_lat, n_ctx):
    T = rq.shape[0]
    C = RET_CHUNK
    n = n_lat + n_ctx

    def fwd(i):
        return jnp.where(i < n_ctx, n_lat + i, i - n_ctx)

    def bwd(i):
        return jnp.where(i < n_ctx, n - 1 - i, n - 1 - i)

    kw, vw = RET_HEADS * RET_DK, RET_W
    spec = lambda w, order: pl.BlockSpec((C, w), lambda i: (order(i), 0))
    dec, xi, zeta, cd = consts
    return pl.pallas_call(
        _ret_kernel,
        out_shape=(jax.ShapeDtypeStruct((T, vw), F32), jax.ShapeDtypeStruct((T, vw), F32)),
        grid=(n,),
        in_specs=[spec(kw, fwd), spec(kw, fwd), spec(vw, fwd),
                  spec(kw, bwd), spec(kw, bwd), spec(vw, bwd),
                  _const_spec(dec.shape), _const_spec(xi.shape),
                  _const_spec(zeta.shape), _const_spec(cd.shape)],
        out_specs=(spec(vw, fwd), spec(vw, bwd)),
        scratch_shapes=[pltpu.VMEM((RET_HEADS, RET_DK, RET_DV), F32),
                        pltpu.VMEM((RET_HEADS, RET_DK, RET_DV), F32)],
        compiler_params=_params("arbitrary"),
        name="retention",
    )(rq, rk, rv, rq, rk, rv, dec, xi, zeta, cd)


def _ret_consts():
    C = RET_CHUNK
    h = jnp.arange(RET_HEADS, dtype=F32)
    pos = jnp.arange(C, dtype=F32)
    diff = pos[:, None] - pos[None, :]

    def one(offset, reverse):
        lg = jnp.log1p(-jnp.exp2(-(RET_DECAY_START + offset) - h))
        dd = -diff if reverse else diff
        keep = dd >= 0
        dec = jnp.where(keep[None], jnp.exp(lg[:, None, None] * jnp.where(keep, dd, 0.0)[None]), 0.0)
        p = (C - 1 - pos) if reverse else pos
        xi = jnp.exp(lg[:, None] * (p + 1)[None, :])
        zeta = jnp.exp(lg[:, None] * (C - 1 - p)[None, :])
        cd = jnp.exp(lg * C)
        return dec, xi[:, :, None], zeta[:, :, None], jnp.broadcast_to(cd[:, None, None], (RET_HEADS, 1, 1))

    f = one(0.0, False)
    b = one(RET_BWD_OFFSET, True)
    return tuple(jnp.stack([a, c]) for a, c in zip(f, b))


def _flash_kernel(qT_ref, k_ref, vT_ref, o_ref, m_ref, acc_ref, *buf_refs, tk, nk, lag, unroll, sub):
    n_buf = lag + 1
    s_bufs, c_bufs = buf_refs[:n_buf], buf_refs[n_buf:]
    q = qT_ref[0]
    m_ref[...] = jnp.full_like(m_ref, NEG_BIG)
    acc_ref[...] = jnp.zeros_like(acc_ref)
    n_sub = tk // sub

    def key_rows(j, r):
        base = j * tk if isinstance(j, int) else pl.multiple_of(j * tk, tk)
        return pl.ds(base + r * sub, sub)

    def step(j, phase, do_scores, do_values):
        s_cur, cmax_cur = s_bufs[phase], c_bufs[phase]
        s_nxt, cmax_nxt = s_bufs[(phase + lag) % n_buf], c_bufs[(phase + lag) % n_buf]
        if do_values:
            m_prev = m_ref[...]
            m_new = jnp.maximum(m_prev, cmax_cur[...])
            alpha = jnp.exp2(m_prev - m_new)
            m_ref[...] = m_new
        cmax = pv = None
        for r in range(n_sub):
            rows = pl.ds(r * sub, sub)
            if do_values:
                p = jnp.exp2(s_cur[rows, :] - m_new).astype(BF16)
                part = _dot(vT_ref[0, :, key_rows(j, r)], p)
                pv = part if pv is None else pv + part
            if do_scores:
                s = _dot(k_ref[key_rows(j + lag, r), :], q)
                s_nxt[rows, :] = s
                part = jnp.max(s, axis=0, keepdims=True)
                cmax = part if cmax is None else jnp.maximum(cmax, part)
        if do_scores:
            cmax_nxt[...] = cmax
        if do_values:
            acc_ref[...] = alpha * acc_ref[...] + pv

    def body(i, carry):
        for u in range(unroll):
            step(i * unroll + u, u % n_buf, True, True)
        return carry

    for j in range(-lag, 0):
        step(j, j % n_buf, j + lag < nk, False)
    n_body = max(nk - lag, 0) // unroll
    if n_body < 2:
        n_body = 0
    if n_body:
        lax.fori_loop(0, n_body, body, 0)
    for j in range(n_body * unroll, nk):
        step(j, j % n_buf, j + lag < nk, True)
    acc = acc_ref[...]
    o_ref[...] = (acc[:HEAD_LANES] / acc[HEAD_LANES:HEAD_LANES + 1]).T.astype(o_ref.dtype)


def _flash_kernel_into(qT_ref, k_ref, vT_ref, y_hbm_ref, o_ref, *scratch, **kw):
    del y_hbm_ref
    _flash_kernel(qT_ref, k_ref, vT_ref, o_ref, *scratch, **kw)


def _flash_call(qT, k, vT, y_prev, *, group, q_rows, q_off, kv_rows, kv_off, tq, tk, name):
    hq, T = qT.shape[0], qT.shape[2]
    sub = min(tk, FLASH_SUB)
    n_buf = FLASH_LAG + 1
    assert q_rows % tq == 0 and q_off % tq == 0 and kv_rows % tk == 0 and tk % sub == 0
    assert FLASH_UNROLL % n_buf == 0
    nq, nk = q_rows // tq, kv_rows // tk
    qb, kb = q_off // tq, kv_off // kv_rows
    kw = dict(tk=tk, nk=nk, lag=FLASH_LAG, unroll=FLASH_UNROLL, sub=sub)
    in_specs = [
        pl.BlockSpec((1, HEAD_LANES, tq), lambda h, i: (h, 0, i + qb)),
        pl.BlockSpec((kv_rows, HEAD_LANES), lambda h, i: (kb, h // group)),
        pl.BlockSpec((1, V_ROWS, kv_rows), lambda h, i: (h // group, 0, kb)),
    ]
    operands = (qT, k, vT)
    if y_prev is not None:
        in_specs.append(pl.BlockSpec(memory_space=pl.ANY))
        operands += (y_prev,)
    return pl.pallas_call(
        functools.partial(_flash_kernel if y_prev is None else _flash_kernel_into, **kw),
        out_shape=jax.ShapeDtypeStruct((T, hq * HEAD_LANES), BF16),
        grid=(hq, nq),
        in_specs=in_specs,
        out_specs=pl.BlockSpec((tq, HEAD_LANES), lambda h, i: (i + qb, h)),
        scratch_shapes=[pltpu.VMEM((1, tq), F32), pltpu.VMEM((V_ROWS, tq), F32)]
                       + [pltpu.VMEM((tk, tq), F32)] * n_buf + [pltpu.VMEM((1, tq), F32)] * n_buf,
        input_output_aliases={} if y_prev is None else {3: 0},
        compiler_params=_params("arbitrary", "arbitrary"),
        name=name,
    )(*operands)


def _pick_tile(n, candidates):
    for t in candidates:
        if n % t == 0:
            return t
    raise ValueError(f"no tile for {n}")


def _attention(qT, k, vT, group, L, Cn, name):
    T = L + Cn
    tq = _pick_tile(L, (1024, 512, 256, 128))
    tk = _pick_tile(T, (1280, 256, 128))
    y = _flash_call(qT, k, vT, None, group=group, q_rows=L, q_off=0, kv_rows=T, kv_off=0,
                    tq=tq, tk=tk, name=name)
    return _flash_call(qT, k, vT, y, group=group, q_rows=Cn, q_off=L, kv_rows=Cn, kv_off=L,
                       tq=Cn, tk=Cn, name=name + "_ctx")


def _merge_kernel(x_ref, mod_ref, gmix_ref, yf_ref, yb_ref, rg_ref, ym_ref, yg_ref,
                  w_gates, w_ro, w_mo, w_go, w_out, o_ref):
    x = x_ref[...]
    mod = mod_ref[0]
    hb = _prenorm(x, mod, gmix_ref[...], 0, 1).astype(BF16)
    gs = jax.nn.sigmoid(_dot(hb, w_gates[...]))
    y = yf_ref[...] + yb_ref[...]
    g = _silu(rg_ref[...])
    parts = []
    for h in range(RET_HEADS):
        sl = slice(h * RET_DV, (h + 1) * RET_DV)
        parts.append((g[:, sl] * _rms(y[:, sl])).astype(BF16))
    y_ret = jnp.concatenate(parts, axis=-1)
    D = D_MODEL
    z = (gs[:, :D] * _dot(y_ret, w_ro[...])
         + gs[:, D:2 * D] * _dot(ym_ref[...], w_mo[...])
         + gs[:, 2 * D:] * _dot(yg_ref[...], w_go[...]))
    o_ref[...] = x + mod[2:3] * _dot(z.astype(BF16), w_out[...])


def _merge_call(xs, mods, gmix, yf, yb, rg, ym, yg, wts, n_lat_tiles, l):
    T, D = xs.shape
    tm = ROW_TILE
    row = lambda w: pl.BlockSpec((tm, w), lambda i: (i, 0))
    return pl.pallas_call(
        _merge_kernel,
        out_shape=jax.ShapeDtypeStruct((T, D), F32),
        grid=(T // tm,),
        in_specs=[row(D), _mod_spec(l, n_lat_tiles), _layer_spec(gmix, l),
                  row(RET_W), row(RET_W), row(RET_W), row(MLA_W), row(GQA_W)]
                 + [_layer_spec(w, l) for w in wts],
        out_specs=row(D),
        compiler_params=_params("arbitrary"),
        name="merge",
    )(xs, mods, gmix, yf, yb, rg, ym, yg, *wts)


def _ffn_kernel(x_ref, mod_ref, g_ref, w_a, w_b, w_o, gf_ref, o_ref, *, final_norm):
    x = x_ref[...]
    mod = mod_ref[0]
    hb = _prenorm(x, mod, g_ref[...], 3, 4).astype(BF16)
    u = (_silu(_dot(hb, w_a[...])) * _dot(hb, w_b[...])).astype(BF16)
    y = x + mod[5:6] * _dot(u, w_o[...])
    o_ref[...] = _rms(y) * gf_ref[...] if final_norm else y


def _ffn_call(xs, mods, g, w_a, w_b, w_o, g_final, n_lat_tiles, l, rows, final_norm):
    D = xs.shape[1]
    tm = ROW_TILE
    row = pl.BlockSpec((tm, D), lambda i: (i, 0))
    return pl.pallas_call(
        functools.partial(_ffn_kernel, final_norm=final_norm),
        out_shape=jax.ShapeDtypeStruct((rows, D), F32),
        grid=(rows // tm,),
        in_specs=[row, _mod_spec(l, n_lat_tiles), _layer_spec(g, l), _layer_spec(w_a, l),
                  _layer_spec(w_b, l), _layer_spec(w_o, l), _const_spec((1, D))],
        out_specs=row,
        compiler_params=_params("arbitrary"),
        name="ffn",
    )(xs, mods, g, w_a, w_b, w_o, g_final)


def _rope_tables(L, Cn):
    n_rows = L // GRID_W

    def angles(n, dim):
        half = dim // 2
        inv_freq = ROPE_BASE ** (-jnp.arange(half, dtype=F32) / half)
        ang = jnp.arange(n, dtype=jnp.int32).astype(F32)[:, None] * inv_freq[None, :]
        return jnp.cos(ang), jnp.sin(ang)

    def tables(dim, lane0):
        cr, sr = (jnp.repeat(a, GRID_W, axis=0) for a in angles(n_rows, dim // 2))
        cc, sc = (jnp.tile(a, (n_rows, 1)) for a in angles(GRID_W, dim // 2))
        z = jnp.zeros_like(sr)
        c = jnp.concatenate([cr, cr, cc, cc], axis=1)
        sa = jnp.concatenate([-sr, z, -sc, z], axis=1)
        sb = jnp.concatenate([z, sr, z, sc], axis=1)
        pad = lambda a, fill: jnp.concatenate(
            [jnp.full((L, lane0), fill, F32), a, jnp.full((L, HEAD_LANES - lane0 - dim), fill, F32)], axis=1)
        c, sa, sb = pad(c, 1.0), pad(sa, 0.0), pad(sb, 0.0)
        ctx = lambda fill: jnp.full((Cn, HEAD_LANES), fill, F32)
        return (jnp.concatenate([c, ctx(1.0)]), jnp.concatenate([sa, ctx(0.0)]),
                jnp.concatenate([sb, ctx(0.0)]))

    return tables(RET_DK, 0) + tables(MLA_ROPE, MLA_NOPE)


def _stacked_weights(w_in, w_mla_qb, w_mla_kvb):
    depth = w_in.shape[0]
    split_at = np.cumsum(IN_SIZES)[:-1].tolist()
    rq, rk, rv, rg, cq, ckv, kr, gq, gk, gv, gates = jnp.split(w_in.astype(BF16), split_at, axis=-1)
    kr_pad = jnp.pad(kr, ((0, 0), (0, 0), (MLA_NOPE, HEAD_LANES - MLA_NOPE - MLA_ROPE)))
    qb = w_mla_qb.astype(BF16).reshape(depth, MLA_Q_RANK, MLA_HEADS, MLA_NOPE + MLA_ROPE)
    qb = jnp.pad(qb, ((0, 0), (0, 0), (0, 0), (0, HEAD_LANES - MLA_NOPE - MLA_ROPE)))
    kvb = w_mla_kvb.astype(BF16).reshape(depth, MLA_KV_RANK, MLA_HEADS, MLA_NOPE + MLA_V)
    kb = jnp.pad(kvb[..., :MLA_NOPE], ((0, 0), (0, 0), (0, 0), (0, HEAD_LANES - MLA_NOPE)))
    flat = lambda w: w.reshape(depth, w.shape[1], -1)
    return (rq, rk, rv, rg, cq, ckv, kr_pad, gq, gk, gv), (flat(qb), flat(kb), flat(kvb[..., MLA_NOPE:])), gates


def kernel(x, c, ctx, c_ctx, w_mod, b_mod, g_mix, w_in, g_mla_q, g_mla_kv, w_mla_qb, w_mla_kvb,
           g_gqa_q, g_gqa_k, w_ret_o, w_mla_o, w_gqa_o, w_out, g_ffn, w_ffn_in, w_ffn_out, g_final):
    B, L, D = x.shape
    Cn = ctx.shape[1]
    depth = w_mod.shape[0]
    assert B == 1 and D == D_MODEL and L % ROW_TILE == 0 and Cn % ROW_TILE == 0 and L % Cn == 0
    assert L % GRID_W == 0
    n_lat_tiles = L // ROW_TILE
    ffn_hidden = w_ffn_out.shape[1]

    xs = jnp.concatenate([x[0], ctx[0]], axis=0)
    cond = jnp.zeros((MOD_ROWS, D), F32).at[0].set(c[0]).at[1].set(c_ctx)
    mods = _mod_call(cond, w_mod, b_mod)
    mods = mods[:, :2].reshape(depth, 2, N_MOD, D)
    mods = jnp.pad(mods, ((0, 0), (0, 0), (0, MOD_ROWS - N_MOD), (0, 0)))

    tabs = _rope_tables(L, Cn)
    ret_consts = _ret_consts()
    rows3 = lambda g: g.reshape(depth, 1, -1)
    in_w, mla_w, w_gates = _stacked_weights(w_in, w_mla_qb, w_mla_kvb)
    wts = in_w + (rows3(g_mla_q), rows3(g_mla_kv)) + mla_w + (rows3(g_gqa_q), rows3(g_gqa_k))
    merge_w = (w_gates,) + tuple(w.astype(BF16) for w in (w_ret_o, w_mla_o, w_gqa_o, w_out))
    w_ffn = w_ffn_in.astype(BF16)
    w_ffn_a, w_ffn_b, w_ffn_o = w_ffn[..., :ffn_hidden], w_ffn[..., ffn_hidden:], w_ffn_out.astype(BF16)
    g_mix3, g_ffn3 = rows3(g_mix), rows3(g_ffn)

    for l in range(depth):
        rq, rk, rv, rg, mqT, mk, mvT, gqT, gk, gvT = _inproj_call(xs, mods, g_mix3, tabs, wts, n_lat_tiles, l)
        yf, yb = _ret_call(rq, rk, rv, ret_consts, L // RET_CHUNK, Cn // RET_CHUNK)
        ym = _attention(mqT, mk, mvT, 1, L, Cn, "mla")
        yg = _attention(gqT, gk, gvT, GQA_HEADS // GQA_KV_HEADS, L, Cn, "gqa")
        xs = _merge_call(xs, mods, g_mix3, yf, yb, rg, ym, yg, merge_w, n_lat_tiles, l)
        last = l == depth - 1
        xs = _ffn_call(xs, mods, g_ffn3, w_ffn_a, w_ffn_b, w_ffn_o, g_final.reshape(1, -1), n_lat_tiles, l,
                       rows=L if last else L + Cn, final_norm=last)
    return xs[None]
```

```python
import functools
import math

import jax
import jax.numpy as jnp
import numpy as np
from jax import lax
from jax.experimental import pallas as pl
from jax.experimental.pallas import tpu as pltpu

D_MODEL = 1024
GRID_W = 64
RET_CHUNK = 128
ROPE_BASE = 10000.0
NORM_EPS = 1e-6
N_MOD = 6

RET_HEADS = 4
RET_DK = 128
RET_DV = 256
RET_DECAY_START = 5.0
RET_BWD_OFFSET = 0.5

MLA_HEADS = 8
MLA_Q_RANK = 256
MLA_KV_RANK = 256
MLA_NOPE = 64
MLA_ROPE = 32
MLA_V = 128

GQA_HEADS = 8
GQA_KV_HEADS = 2
GQA_HD = 128

RET_W = RET_HEADS * RET_DV
MLA_W = MLA_HEADS * MLA_V
GQA_W = GQA_HEADS * GQA_HD
N_BRANCH = 3

IN_SIZES = (
    RET_HEADS * RET_DK, RET_HEADS * RET_DK, RET_W, RET_W,
    MLA_Q_RANK, MLA_KV_RANK, MLA_ROPE,
    GQA_W, GQA_KV_HEADS * GQA_HD, GQA_KV_HEADS * GQA_HD,
    N_BRANCH * D_MODEL,
)

HEAD_LANES = 128
BF16_SUBLANES = 16
V_ROWS = HEAD_LANES + BF16_SUBLANES
MOD_ROWS = 8
ROW_TILE = 256
FLASH_LAG = 2
FLASH_UNROLL = 3
FLASH_SUB = 256
V7X_VMEM_LIMIT = 56 * 1024 * 1024
LOG2E = math.log2(math.e)
NEG_BIG = -1e30

F32 = jnp.float32
BF16 = jnp.bfloat16


def _params(*sem, flags=None):
    return pltpu.CompilerParams(dimension_semantics=sem, vmem_limit_bytes=V7X_VMEM_LIMIT, flags=flags)


def _const_spec(shape):
    zeros = (0,) * len(shape)
    return pl.BlockSpec(shape, lambda *_: zeros)


def _layer_spec(w, l):
    zeros = (0,) * (w.ndim - 1)
    return pl.BlockSpec((None,) + w.shape[1:], lambda *_: (l,) + zeros)


def _mod_spec(l, n_lat_tiles):
    return pl.BlockSpec((None, 1, MOD_ROWS, D_MODEL), lambda i: (l, i // n_lat_tiles, 0, 0))


def _dot(a, b):
    return jnp.dot(a, b, preferred_element_type=F32)


def _rms(x):
    return x * lax.rsqrt(jnp.mean(x * x, axis=-1, keepdims=True) + NORM_EPS)


def _silu(x):
    return x * jax.nn.sigmoid(x)


def _rope(x, c, sa, sb, block):
    up = pltpu.roll(x, HEAD_LANES - block, 1)
    dn = pltpu.roll(x, block, 1)
    return x * c + up * sa + dn * sb


def _head(x, h):
    return x[:, h * HEAD_LANES:(h + 1) * HEAD_LANES]


def _mod_kernel(cond_ref, w_ref, b_ref, o_ref):
    s = _silu(cond_ref[...]).astype(BF16)
    o_ref[0] = _dot(s, w_ref[0].astype(BF16)) + b_ref[0]


def _mod_call(cond, w_mod, b_mod):
    depth, d, n = w_mod.shape
    tn = n // 4
    return pl.pallas_call(
        _mod_kernel,
        out_shape=jax.ShapeDtypeStruct((depth, MOD_ROWS, n), F32),
        grid=(depth, n // tn),
        in_specs=[
            pl.BlockSpec((MOD_ROWS, d), lambda l, j: (0, 0)),
            pl.BlockSpec((1, d, tn), lambda l, j: (l, 0, j)),
            pl.BlockSpec((1, 1, tn), lambda l, j: (l, 0, j)),
        ],
        out_specs=pl.BlockSpec((1, MOD_ROWS, tn), lambda l, j: (l, 0, j)),
        compiler_params=_params("arbitrary", "arbitrary"),
        name="mod",
    )(cond, w_mod, b_mod.reshape(depth, 1, n))


def _prenorm(x, mod, g, shift_row, scale_row):
    h = _rms(x) * g
    return h * (1.0 + mod[scale_row:scale_row + 1]) + mod[shift_row:shift_row + 1]


def _rms_rows(x):
    return x * lax.rsqrt(jnp.mean(x * x, axis=0, keepdims=True) + NORM_EPS)


def _rope_rows(x, c, sa, sb, block):
    up = jnp.concatenate([x[block:], x[:block]], axis=0)
    dn = jnp.concatenate([x[-block:], x[:-block]], axis=0)
    return x * c + up * sa + dn * sb


def _inproj_kernel(x_ref, mod_ref, gmix_ref, c_ref, sa_ref, sb_ref, cm_ref, sam_ref, sbm_ref,
                   cT_ref, saT_ref, sbT_ref, cmT_ref, samT_ref, sbmT_ref,
                   w_rq, w_rk, w_rv, w_rg, w_ckv, w_kr, w_gk, w_kb,
                   w_cqT, w_qbT, w_vbT, w_gqT, w_gvT, g_mkv, g_gk, g_mqT, g_gqT,
                   rq_o, rk_o, rv_o, rg_o, mqT_o, mk_o, mvT_o, gqT_o, gk_o, gvT_o):
    h = _prenorm(x_ref[...], mod_ref[0], gmix_ref[...], 0, 1)
    hb = h.astype(BF16)
    hTb = h.T.astype(BF16)
    c, sa, sb = c_ref[...], sa_ref[...], sb_ref[...]
    cm, sam, sbm = cm_ref[...], sam_ref[...], sbm_ref[...]
    rope_big = lambda v: _rope(v, c, sa, sb, RET_DK // 4)
    rope_mla = lambda v: _rope(v, cm, sam, sbm, MLA_ROPE // 4)
    rows = lambda v, i: v[i * HEAD_LANES:(i + 1) * HEAD_LANES]
    ones_rows = (lax.broadcasted_iota(jnp.int32, (BF16_SUBLANES, x_ref.shape[0]), 0) == 0).astype(BF16)

    rq = _dot(hb, w_rq[...])
    rk = _dot(hb, w_rk[...])
    for i in range(RET_HEADS):
        sl = slice(i * HEAD_LANES, (i + 1) * HEAD_LANES)
        rq_o[:, sl] = rope_big(_head(rq, i))
        rk_o[:, sl] = rope_big(_head(rk, i)) * (RET_DK ** -0.5)
    rv_o[...] = _dot(hb, w_rv[...]).astype(BF16)
    rg_o[...] = _dot(hb, w_rg[...])

    mla_qscale = (MLA_NOPE + MLA_ROPE) ** -0.5 * LOG2E
    cqnT = (_rms_rows(_dot(w_cqT[...], hTb)) * g_mqT[...]).astype(BF16)
    qT = _dot(w_qbT[...], cqnT)
    cmT, samT, sbmT = cmT_ref[...], samT_ref[...], sbmT_ref[...]
    for i in range(MLA_HEADS):
        mqT_o[i] = (_rope_rows(rows(qT, i), cmT, samT, sbmT, MLA_ROPE // 4) * mla_qscale).astype(BF16)
    ckvn = _rms(_dot(hb, w_ckv[...])) * g_mkv[...]
    kn = _dot(ckvn.astype(BF16), w_kb[...])
    kr = rope_mla(_dot(hb, w_kr[...]))
    vT = _dot(w_vbT[...], ckvn.T.astype(BF16))
    for i in range(MLA_HEADS):
        sl = slice(i * HEAD_LANES, (i + 1) * HEAD_LANES)
        mk_o[:, sl] = (_head(kn, i) + kr).astype(BF16)
        mvT_o[i, :HEAD_LANES, :] = rows(vT, i).astype(BF16)
        mvT_o[i, HEAD_LANES:, :] = ones_rows

    gqa_qscale = GQA_HD ** -0.5 * LOG2E
    gqT = _dot(w_gqT[...], hTb)
    cT, saT, sbT = cT_ref[...], saT_ref[...], sbT_ref[...]
    for i in range(GQA_HEADS):
        qn = _rms_rows(rows(gqT, i)) * g_gqT[...]
        gqT_o[i] = (_rope_rows(qn, cT, saT, sbT, GQA_HD // 4) * gqa_qscale).astype(BF16)
    gk = _dot(hb, w_gk[...])
    gvT = _dot(w_gvT[...], hTb)
    for i in range(GQA_KV_HEADS):
        sl = slice(i * HEAD_LANES, (i + 1) * HEAD_LANES)
        gk_o[:, sl] = rope_big(_rms(_head(gk, i)) * g_gk[...]).astype(BF16)
        gvT_o[i, :HEAD_LANES, :] = rows(gvT, i).astype(BF16)
        gvT_o[i, HEAD_LANES:, :] = ones_rows


def _inproj_call(xs, mods, gmix, tabs, tabsT, wts, n_lat_tiles, l):
    T, D = xs.shape
    tm = ROW_TILE
    row = lambda w: pl.BlockSpec((tm, w), lambda i: (i, 0))
    col = pl.BlockSpec((HEAD_LANES, tm), lambda i: (0, i))
    headsT = lambda nh, rows=HEAD_LANES: pl.BlockSpec((nh, rows, tm), lambda i: (0, 0, i))
    in_specs = [row(D), _mod_spec(l, n_lat_tiles), _layer_spec(gmix, l)]
    in_specs += [row(HEAD_LANES)] * 6 + [col] * 6 + [_layer_spec(w, l) for w in wts]
    out_shape = (
        jax.ShapeDtypeStruct((T, RET_HEADS * RET_DK), F32),
        jax.ShapeDtypeStruct((T, RET_HEADS * RET_DK), F32),
        jax.ShapeDtypeStruct((T, RET_W), BF16),
        jax.ShapeDtypeStruct((T, RET_W), F32),
        jax.ShapeDtypeStruct((MLA_HEADS, HEAD_LANES, T), BF16),
        jax.ShapeDtypeStruct((T, MLA_HEADS * HEAD_LANES), BF16),
        jax.ShapeDtypeStruct((MLA_HEADS, V_ROWS, T), BF16),
        jax.ShapeDtypeStruct((GQA_HEADS, HEAD_LANES, T), BF16),
        jax.ShapeDtypeStruct((T, GQA_KV_HEADS * GQA_HD), BF16),
        jax.ShapeDtypeStruct((GQA_KV_HEADS, V_ROWS, T), BF16),
    )
    out_specs = (
        row(RET_HEADS * RET_DK), row(RET_HEADS * RET_DK), row(RET_W), row(RET_W),
        headsT(MLA_HEADS), row(MLA_HEADS * HEAD_LANES), headsT(MLA_HEADS, V_ROWS),
        headsT(GQA_HEADS), row(GQA_KV_HEADS * GQA_HD), headsT(GQA_KV_HEADS, V_ROWS),
    )
    return pl.pallas_call(
        _inproj_kernel,
        out_shape=out_shape,
        grid=(T // tm,),
        in_specs=in_specs,
        out_specs=out_specs,
        compiler_params=_params("arbitrary"),
        name="inproj",
    )(xs, mods, gmix, *tabs, *tabsT, *wts)


def _ret_kernel(qf_ref, kf_ref, vf_ref, qb_ref, kb_ref, vb_ref,
                dec_ref, xi_ref, zeta_ref, cd_ref,
                yf_ref, yb_ref, sf_ref, sb_ref):
    @pl.when(pl.program_id(0) == 0)
    def _():
        sf_ref[...] = jnp.zeros_like(sf_ref)
        sb_ref[...] = jnp.zeros_like(sb_ref)

    dirs = ((qf_ref, kf_ref, vf_ref, yf_ref, sf_ref), (qb_ref, kb_ref, vb_ref, yb_ref, sb_ref))
    pending = []
    for d, (q_ref, k_ref, v_ref, y_ref, s_ref) in enumerate(dirs):
        for h in range(RET_HEADS):
            ksl = slice(h * RET_DK, (h + 1) * RET_DK)
            vsl = slice(h * RET_DV, (h + 1) * RET_DV)
            k = k_ref[:, ksl]
            v = v_ref[:, vsl]
            qb16 = q_ref[:, ksl].astype(BF16)
            scores = lax.dot_general(qb16, k.astype(BF16), (((1,), (1,)), ((), ())),
                                     preferred_element_type=F32) * dec_ref[d, h]
            state = s_ref[h]
            cross = _dot(qb16, state.astype(BF16)) * xi_ref[d, h]
            kz = (k * zeta_ref[d, h]).T.astype(BF16)
            s_ref[h] = cd_ref[d, h] * state + _dot(kz, v)
            pending.append((y_ref, vsl, scores.astype(BF16), v, cross))
    for y_ref, vsl, scores, v, cross in pending:
        y_ref[:, vsl] = _dot(scores, v) + cross


def _ret_call(rq, rk, rv, consts, n_lat, n_ctx):
    T = rq.shape[0]
    C = RET_CHUNK
    n = n_lat + n_ctx

    def fwd(i):
        return jnp.where(i < n_ctx, n_lat + i, i - n_ctx)

    def bwd(i):
        return jnp.where(i < n_ctx, n - 1 - i, n - 1 - i)

    kw, vw = RET_HEADS * RET_DK, RET_W
    spec = lambda w, order: pl.BlockSpec((C, w), lambda i: (order(i), 0))
    dec, xi, zeta, cd = consts
    return pl.pallas_call(
        _ret_kernel,
        out_shape=(jax.ShapeDtypeStruct((T, vw), F32), jax.ShapeDtypeStruct((T, vw), F32)),
        grid=(n,),
        in_specs=[spec(kw, fwd), spec(kw, fwd), spec(vw, fwd),
                  spec(kw, bwd), spec(kw, bwd), spec(vw, bwd),
                  _const_spec(dec.shape), _const_spec(xi.shape),
                  _const_spec(zeta.shape), _const_spec(cd.shape)],
        out_specs=(spec(vw, fwd), spec(vw, bwd)),
        scratch_shapes=[pltpu.VMEM((RET_HEADS, RET_DK, RET_DV), F32),
                        pltpu.VMEM((RET_HEADS, RET_DK, RET_DV), F32)],
        compiler_params=_params("arbitrary"),
        name="retention",
    )(rq, rk, rv, rq, rk, rv, dec, xi, zeta, cd)


def _ret_consts():
    C = RET_CHUNK
    h = jnp.arange(RET_HEADS, dtype=F32)
    pos = jnp.arange(C, dtype=F32)
    diff = pos[:, None] - pos[None, :]

    def one(offset, reverse):
        lg = jnp.log1p(-jnp.exp2(-(RET_DECAY_START + offset) - h))
        dd = -diff if reverse else diff
        keep = dd >= 0
        dec = jnp.where(keep[None], jnp.exp(lg[:, None, None] * jnp.where(keep, dd, 0.0)[None]), 0.0)
        p = (C - 1 - pos) if reverse else pos
        xi = jnp.exp(lg[:, None] * (p + 1)[None, :])
        zeta = jnp.exp(lg[:, None] * (C - 1 - p)[None, :])
        cd = jnp.exp(lg * C)
        return dec, xi[:, :, None], zeta[:, :, None], jnp.broadcast_to(cd[:, None, None], (RET_HEADS, 1, 1))

    f = one(0.0, False)
    b = one(RET_BWD_OFFSET, True)
    return tuple(jnp.stack([a, c]) for a, c in zip(f, b))


def _flash_kernel(qT_ref, k_ref, vT_ref, o_ref, m_ref, acc_ref, *buf_refs, tk, nk, lag, unroll, sub):
    n_buf = lag + 1
    s_bufs, c_bufs = buf_refs[:n_buf], buf_refs[n_buf:]
    q = qT_ref[0]
    m_ref[...] = jnp.full_like(m_ref, NEG_BIG)
    acc_ref[...] = jnp.zeros_like(acc_ref)
    n_sub = tk // sub

    def key_rows(j, r):
        base = j * tk if isinstance(j, int) else pl.multiple_of(j * tk, tk)
        return pl.ds(base + r * sub, sub)

    def step(j, phase, do_scores, do_values):
        s_cur, cmax_cur = s_bufs[phase], c_bufs[phase]
        s_nxt, cmax_nxt = s_bufs[(phase + lag) % n_buf], c_bufs[(phase + lag) % n_buf]
        if do_values:
            m_prev = m_ref[...]
            m_new = jnp.maximum(m_prev, cmax_cur[...])
            alpha = jnp.exp2(m_prev - m_new)
            m_ref[...] = m_new
        cmax = pv = None
        for r in range(n_sub):
            rows = pl.ds(r * sub, sub)
            if do_values:
                p = jnp.exp2(s_cur[rows, :] - m_new).astype(BF16)
                part = _dot(vT_ref[0, :, key_rows(j, r)], p)
                pv = part if pv is None else pv + part
            if do_scores:
                s = _dot(k_ref[key_rows(j + lag, r), :], q)
                s_nxt[rows, :] = s
                part = jnp.max(s, axis=0, keepdims=True)
                cmax = part if cmax is None else jnp.maximum(cmax, part)
        if do_scores:
            cmax_nxt[...] = cmax
        if do_values:
            acc_ref[...] = alpha * acc_ref[...] + pv

    def body(i, carry):
        for u in range(unroll):
            step(i * unroll + u, u % n_buf, True, True)
        return carry

    for j in range(-lag, 0):
        step(j, j % n_buf, j + lag < nk, False)
    n_body = max(nk - lag, 0) // unroll
    if n_body < 2:
        n_body = 0
    if n_body:
        lax.fori_loop(0, n_body, body, 0)
    for j in range(n_body * unroll, nk):
        step(j, j % n_buf, j + lag < nk, True)
    acc = acc_ref[...]
    o_ref[...] = (acc[:HEAD_LANES] / acc[HEAD_LANES:HEAD_LANES + 1]).T.astype(o_ref.dtype)


def _flash_kernel_into(qT_ref, k_ref, vT_ref, y_hbm_ref, o_ref, *scratch, **kw):
    del y_hbm_ref
    _flash_kernel(qT_ref, k_ref, vT_ref, o_ref, *scratch, **kw)


def _flash_call(qT, k, vT, y_prev, *, group, q_rows, q_off, kv_rows, kv_off, tq, tk, name):
    hq, T = qT.shape[0], qT.shape[2]
    sub = min(tk, FLASH_SUB)
    n_buf = FLASH_LAG + 1
    assert q_rows % tq == 0 and q_off % tq == 0 and kv_rows % tk == 0 and tk % sub == 0
    assert FLASH_UNROLL % n_buf == 0
    nq, nk = q_rows // tq, kv_rows // tk
    qb, kb = q_off // tq, kv_off // kv_rows
    kw = dict(tk=tk, nk=nk, lag=FLASH_LAG, unroll=FLASH_UNROLL, sub=sub)
    in_specs = [
        pl.BlockSpec((1, HEAD_LANES, tq), lambda h, i: (h, 0, i + qb)),
        pl.BlockSpec((kv_rows, HEAD_LANES), lambda h, i: (kb, h // group)),
        pl.BlockSpec((1, V_ROWS, kv_rows), lambda h, i: (h // group, 0, kb)),
    ]
    operands = (qT, k, vT)
    if y_prev is not None:
        in_specs.append(pl.BlockSpec(memory_space=pl.ANY))
        operands += (y_prev,)
    return pl.pallas_call(
        functools.partial(_flash_kernel if y_prev is None else _flash_kernel_into, **kw),
        out_shape=jax.ShapeDtypeStruct((T, hq * HEAD_LANES), BF16),
        grid=(hq, nq),
        in_specs=in_specs,
        out_specs=pl.BlockSpec((tq, HEAD_LANES), lambda h, i: (i + qb, h)),
        scratch_shapes=[pltpu.VMEM((1, tq), F32), pltpu.VMEM((V_ROWS, tq), F32)]
                       + [pltpu.VMEM((tk, tq), F32)] * n_buf + [pltpu.VMEM((1, tq), F32)] * n_buf,
        input_output_aliases={} if y_prev is None else {3: 0},
        compiler_params=_params("arbitrary", "arbitrary"),
        name=name,
    )(*operands)


def _pick_tile(n, candidates):
    for t in candidates:
        if n % t == 0:
            return t
    raise ValueError(f"no tile for {n}")


def _attention(qT, k, vT, group, L, Cn, name):
    T = L + Cn
    tq = _pick_tile(L, (1024, 512, 256, 128))
    tk = _pick_tile(T, (1280, 256, 128))
    y = _flash_call(qT, k, vT, None, group=group, q_rows=L, q_off=0, kv_rows=T, kv_off=0,
                    tq=tq, tk=tk, name=name)
    return _flash_call(qT, k, vT, y, group=group, q_rows=Cn, q_off=L, kv_rows=Cn, kv_off=L,
                       tq=Cn, tk=Cn, name=name + "_ctx")


def _merge_kernel(x_ref, mod_ref, gmix_ref, yf_ref, yb_ref, rg_ref, ym_ref, yg_ref,
                  w_gates, w_ro, w_mo, w_go, w_out, o_ref):
    x = x_ref[...]
    mod = mod_ref[0]
    hb = _prenorm(x, mod, gmix_ref[...], 0, 1).astype(BF16)
    gs = jax.nn.sigmoid(_dot(hb, w_gates[...]))
    y = yf_ref[...] + yb_ref[...]
    g = _silu(rg_ref[...])
    parts = []
    for h in range(RET_HEADS):
        sl = slice(h * RET_DV, (h + 1) * RET_DV)
        parts.append((g[:, sl] * _rms(y[:, sl])).astype(BF16))
    y_ret = jnp.concatenate(parts, axis=-1)
    D = D_MODEL
    z = (gs[:, :D] * _dot(y_ret, w_ro[...])
         + gs[:, D:2 * D] * _dot(ym_ref[...], w_mo[...])
         + gs[:, 2 * D:] * _dot(yg_ref[...], w_go[...]))
    o_ref[...] = x + mod[2:3] * _dot(z.astype(BF16), w_out[...])


def _merge_call(xs, mods, gmix, yf, yb, rg, ym, yg, wts, n_lat_tiles, l):
    T, D = xs.shape
    tm = ROW_TILE
    row = lambda w: pl.BlockSpec((tm, w), lambda i: (i, 0))
    return pl.pallas_call(
        _merge_kernel,
        out_shape=jax.ShapeDtypeStruct((T, D), F32),
        grid=(T // tm,),
        in_specs=[row(D), _mod_spec(l, n_lat_tiles), _layer_spec(gmix, l),
                  row(RET_W), row(RET_W), row(RET_W), row(MLA_W), row(GQA_W)]
                 + [_layer_spec(w, l) for w in wts],
        out_specs=row(D),
        compiler_params=_params("arbitrary"),
        name="merge",
    )(xs, mods, gmix, yf, yb, rg, ym, yg, *wts)


def _ffn_kernel(x_ref, mod_ref, g_ref, w_a, w_b, w_o, gf_ref, o_ref, *, final_norm):
    x = x_ref[...]
    mod = mod_ref[0]
    hb = _prenorm(x, mod, g_ref[...], 3, 4).astype(BF16)
    u = (_silu(_dot(hb, w_a[...])) * _dot(hb, w_b[...])).astype(BF16)
    y = x + mod[5:6] * _dot(u, w_o[...])
    o_ref[...] = _rms(y) * gf_ref[...] if final_norm else y


def _ffn_call(xs, mods, g, w_a, w_b, w_o, g_final, n_lat_tiles, l, rows, final_norm):
    D = xs.shape[1]
    tm = ROW_TILE
    row = pl.BlockSpec((tm, D), lambda i: (i, 0))
    return pl.pallas_call(
        functools.partial(_ffn_kernel, final_norm=final_norm),
        out_shape=jax.ShapeDtypeStruct((rows, D), F32),
        grid=(rows // tm,),
        in_specs=[row, _mod_spec(l, n_lat_tiles), _layer_spec(g, l), _layer_spec(w_a, l),
                  _layer_spec(w_b, l), _layer_spec(w_o, l), _const_spec((1, D))],
        out_specs=row,
        compiler_params=_params("arbitrary"),
        name="ffn",
    )(xs, mods, g, w_a, w_b, w_o, g_final)


def _rope_tables(L, Cn):
    n_rows = L // GRID_W

    def angles(n, dim):
        half = dim // 2
        inv_freq = ROPE_BASE ** (-jnp.arange(half, dtype=F32) / half)
        ang = jnp.arange(n, dtype=jnp.int32).astype(F32)[:, None] * inv_freq[None, :]
        return jnp.cos(ang), jnp.sin(ang)

    def tables(dim, lane0):
        cr, sr = (jnp.repeat(a, GRID_W, axis=0) for a in angles(n_rows, dim // 2))
        cc, sc = (jnp.tile(a, (n_rows, 1)) for a in angles(GRID_W, dim // 2))
        z = jnp.zeros_like(sr)
        c = jnp.concatenate([cr, cr, cc, cc], axis=1)
        sa = jnp.concatenate([-sr, z, -sc, z], axis=1)
        sb = jnp.concatenate([z, sr, z, sc], axis=1)
        pad = lambda a, fill: jnp.concatenate(
            [jnp.full((L, lane0), fill, F32), a, jnp.full((L, HEAD_LANES - lane0 - dim), fill, F32)], axis=1)
        c, sa, sb = pad(c, 1.0), pad(sa, 0.0), pad(sb, 0.0)
        ctx = lambda fill: jnp.full((Cn, HEAD_LANES), fill, F32)
        return (jnp.concatenate([c, ctx(1.0)]), jnp.concatenate([sa, ctx(0.0)]),
                jnp.concatenate([sb, ctx(0.0)]))

    return tables(RET_DK, 0) + tables(MLA_ROPE, MLA_NOPE)


def _stacked_weights(w_in, w_mla_qb, w_mla_kvb):
    depth = w_in.shape[0]
    split_at = np.cumsum(IN_SIZES)[:-1].tolist()
    rq, rk, rv, rg, cq, ckv, kr, gq, gk, gv, gates = jnp.split(w_in.astype(BF16), split_at, axis=-1)
    kr_pad = jnp.pad(kr, ((0, 0), (0, 0), (MLA_NOPE, HEAD_LANES - MLA_NOPE - MLA_ROPE)))
    qb = w_mla_qb.astype(BF16).reshape(depth, MLA_Q_RANK, MLA_HEADS, MLA_NOPE + MLA_ROPE)
    qb = jnp.pad(qb, ((0, 0), (0, 0), (0, 0), (0, HEAD_LANES - MLA_NOPE - MLA_ROPE)))
    kvb = w_mla_kvb.astype(BF16).reshape(depth, MLA_KV_RANK, MLA_HEADS, MLA_NOPE + MLA_V)
    kb = jnp.pad(kvb[..., :MLA_NOPE], ((0, 0), (0, 0), (0, 0), (0, HEAD_LANES - MLA_NOPE)))
    flat = lambda w: w.reshape(depth, w.shape[1], -1)
    wT = lambda w: jnp.swapaxes(w, 1, 2)
    natural = (rq, rk, rv, rg, ckv, kr_pad, gk, flat(kb))
    transposed = (wT(cq), wT(flat(qb)), wT(flat(kvb[..., MLA_NOPE:])), wT(gq), wT(gv))
    return natural, transposed, gates


def kernel(x, c, ctx, c_ctx, w_mod, b_mod, g_mix, w_in, g_mla_q, g_mla_kv, w_mla_qb, w_mla_kvb,
           g_gqa_q, g_gqa_k, w_ret_o, w_mla_o, w_gqa_o, w_out, g_ffn, w_ffn_in, w_ffn_out, g_final):
    B, L, D = x.shape
    Cn = ctx.shape[1]
    depth = w_mod.shape[0]
    assert B == 1 and D == D_MODEL and L % ROW_TILE == 0 and Cn % ROW_TILE == 0 and L % Cn == 0
    assert L % GRID_W == 0
    n_lat_tiles = L // ROW_TILE
    ffn_hidden = w_ffn_out.shape[1]

    xs = jnp.concatenate([x[0], ctx[0]], axis=0)
    cond = jnp.zeros((MOD_ROWS, D), F32).at[0].set(c[0]).at[1].set(c_ctx)
    mods = _mod_call(cond, w_mod, b_mod)
    mods = mods[:, :2].reshape(depth, 2, N_MOD, D)
    mods = jnp.pad(mods, ((0, 0), (0, 0), (0, MOD_ROWS - N_MOD), (0, 0)))

    tabs = _rope_tables(L, Cn)
    ret_consts = _ret_consts()
    rows3 = lambda g: g.reshape(depth, 1, -1)
    w_nat, w_tr, w_gates = _stacked_weights(w_in, w_mla_qb, w_mla_kvb)
    lanes = lambda g: jnp.broadcast_to(g[:, :, None], g.shape + (ROW_TILE,))
    wts = w_nat + w_tr + (rows3(g_mla_kv), rows3(g_gqa_k), lanes(g_mla_q), lanes(g_gqa_q))
    tabsT = tuple(t.T for t in tabs)
    merge_w = (w_gates,) + tuple(w.astype(BF16) for w in (w_ret_o, w_mla_o, w_gqa_o, w_out))
    w_ffn = w_ffn_in.astype(BF16)
    w_ffn_a, w_ffn_b, w_ffn_o = w_ffn[..., :ffn_hidden], w_ffn[..., ffn_hidden:], w_ffn_out.astype(BF16)
    g_mix3, g_ffn3 = rows3(g_mix), rows3(g_ffn)

    for l in range(depth):
        rq, rk, rv, rg, mqT, mk, mvT, gqT, gk, gvT = _inproj_call(
            xs, mods, g_mix3, tabs, tabsT, wts, n_lat_tiles, l)
        yf, yb = _ret_call(rq, rk, rv, ret_consts, L // RET_CHUNK, Cn // RET_CHUNK)
        ym = _attention(mqT, mk, mvT, 1, L, Cn, "mla")
        yg = _attention(gqT, gk, gvT, GQA_HEADS // GQA_KV_HEADS, L, Cn, "gqa")
        xs = _merge_call(xs, mods, g_mix3, yf, yb, rg, ym, yg, merge_w, n_lat_tiles, l)
        last = l == depth - 1
        xs = _ffn_call(xs, mods, g_ffn3, w_ffn_a, w_ffn_b, w_ffn_o, g_final.reshape(1, -1), n_lat_tiles, l,
                       rows=L if last else L + Cn, final_norm=last)
    return xs[None]
```

```python
import functools
import math

import jax
import jax.numpy as jnp
import numpy as np
from jax import lax
from jax.experimental import pallas as pl
from jax.experimental.pallas import tpu as pltpu

D_MODEL = 1024
GRID_W = 64
RET_CHUNK = 128
ROPE_BASE = 10000.0
NORM_EPS = 1e-6
N_MOD = 6

RET_HEADS = 4
RET_DK = 128
RET_DV = 256
RET_DECAY_START = 5.0
RET_BWD_OFFSET = 0.5

MLA_HEADS = 8
MLA_Q_RANK = 256
MLA_KV_RANK = 256
MLA_NOPE = 64
MLA_ROPE = 32
MLA_V = 128

GQA_HEADS = 8
GQA_KV_HEADS = 2
GQA_HD = 128

RET_W = RET_HEADS * RET_DV
MLA_W = MLA_HEADS * MLA_V
GQA_W = GQA_HEADS * GQA_HD
N_BRANCH = 3

IN_SIZES = (
    RET_HEADS * RET_DK, RET_HEADS * RET_DK, RET_W, RET_W,
    MLA_Q_RANK, MLA_KV_RANK, MLA_ROPE,
    GQA_W, GQA_KV_HEADS * GQA_HD, GQA_KV_HEADS * GQA_HD,
    N_BRANCH * D_MODEL,
)

HEAD_LANES = 128
BF16_SUBLANES = 16
V_ROWS = HEAD_LANES + BF16_SUBLANES
MOD_ROWS = 8
ROW_TILE = 256
FLASH_LAG = 2
FLASH_UNROLL = 3
FLASH_SUB = 256
V7X_VMEM_LIMIT = 56 * 1024 * 1024
LOG2E = math.log2(math.e)
NEG_BIG = -1e30

F32 = jnp.float32
BF16 = jnp.bfloat16


def _params(*sem, flags=None):
    return pltpu.CompilerParams(dimension_semantics=sem, vmem_limit_bytes=V7X_VMEM_LIMIT, flags=flags)


def _const_spec(shape):
    zeros = (0,) * len(shape)
    return pl.BlockSpec(shape, lambda *_: zeros)


def _layer_spec(w, l):
    zeros = (0,) * (w.ndim - 1)
    return pl.BlockSpec((None,) + w.shape[1:], lambda *_: (l,) + zeros)


def _mod_spec(l, n_lat_tiles):
    return pl.BlockSpec((None, 1, MOD_ROWS, D_MODEL), lambda i: (l, i // n_lat_tiles, 0, 0))


def _dot(a, b):
    return jnp.dot(a, b, preferred_element_type=F32)


def _rms(x):
    return x * lax.rsqrt(jnp.mean(x * x, axis=-1, keepdims=True) + NORM_EPS)


def _silu(x):
    return x * jax.nn.sigmoid(x)


def _rope(x, c, sa, sb, block):
    up = pltpu.roll(x, HEAD_LANES - block, 1)
    dn = pltpu.roll(x, block, 1)
    return x * c + up * sa + dn * sb


def _head(x, h):
    return x[:, h * HEAD_LANES:(h + 1) * HEAD_LANES]


def _mod_kernel(cond_ref, w_ref, b_ref, o_ref):
    s = _silu(cond_ref[...]).astype(BF16)
    o_ref[0] = _dot(s, w_ref[0].astype(BF16)) + b_ref[0]


def _mod_call(cond, w_mod, b_mod):
    depth, d, n = w_mod.shape
    tn = n // 4
    return pl.pallas_call(
        _mod_kernel,
        out_shape=jax.ShapeDtypeStruct((depth, MOD_ROWS, n), F32),
        grid=(depth, n // tn),
        in_specs=[
            pl.BlockSpec((MOD_ROWS, d), lambda l, j: (0, 0)),
            pl.BlockSpec((1, d, tn), lambda l, j: (l, 0, j)),
            pl.BlockSpec((1, 1, tn), lambda l, j: (l, 0, j)),
        ],
        out_specs=pl.BlockSpec((1, MOD_ROWS, tn), lambda l, j: (l, 0, j)),
        compiler_params=_params("arbitrary", "arbitrary"),
        name="mod",
    )(cond, w_mod, b_mod.reshape(depth, 1, n))


def _prenorm(x, mod, g, shift_row, scale_row):
    h = _rms(x) * g
    return h * (1.0 + mod[scale_row:scale_row + 1]) + mod[shift_row:shift_row + 1]


def _rms_rows(x):
    return x * lax.rsqrt(jnp.mean(x * x, axis=0, keepdims=True) + NORM_EPS)


def _rope_rows(x, c, sa, sb, block):
    up = jnp.concatenate([x[block:], x[:block]], axis=0)
    dn = jnp.concatenate([x[-block:], x[:-block]], axis=0)
    return x * c + up * sa + dn * sb


def _inproj_kernel(x_ref, mod_ref, gmix_ref, c_ref, sa_ref, sb_ref, cm_ref, sam_ref, sbm_ref,
                   cT_ref, saT_ref, sbT_ref, cmT_ref, samT_ref, sbmT_ref,
                   w_rq, w_rk, w_rv, w_rg, w_ckv, w_kr, w_gk, w_kb,
                   w_cqT, w_qbT, w_vbT, w_gqT, w_gvT, g_mkv, g_gk, g_mqT, g_gqT,
                   rq_o, rk_o, rv_o, rg_o, mqT_o, mk_o, mvT_o, gqT_o, gk_o, gvT_o):
    h = _prenorm(x_ref[...], mod_ref[0], gmix_ref[...], 0, 1)
    hb = h.astype(BF16)
    hTb = h.T.astype(BF16)
    c, sa, sb = c_ref[...], sa_ref[...], sb_ref[...]
    cm, sam, sbm = cm_ref[...], sam_ref[...], sbm_ref[...]
    rope_big = lambda v: _rope(v, c, sa, sb, RET_DK // 4)
    rope_mla = lambda v: _rope(v, cm, sam, sbm, MLA_ROPE // 4)
    rows = lambda v, i: v[i * HEAD_LANES:(i + 1) * HEAD_LANES]
    ones_rows = (lax.broadcasted_iota(jnp.int32, (BF16_SUBLANES, x_ref.shape[0]), 0) == 0).astype(BF16)

    rq = _dot(hb, w_rq[...])
    rk = _dot(hb, w_rk[...])
    for i in range(RET_HEADS):
        sl = slice(i * HEAD_LANES, (i + 1) * HEAD_LANES)
        rq_o[:, sl] = rope_big(_head(rq, i))
        rk_o[:, sl] = rope_big(_head(rk, i)) * (RET_DK ** -0.5)
    rv_o[...] = _dot(hb, w_rv[...]).astype(BF16)
    rg_o[...] = _dot(hb, w_rg[...])

    mla_qscale = (MLA_NOPE + MLA_ROPE) ** -0.5 * LOG2E
    cqnT = (_rms_rows(_dot(w_cqT[...], hTb)) * g_mqT[...]).astype(BF16)
    qT = _dot(w_qbT[...], cqnT)
    cmT, samT, sbmT = cmT_ref[...], samT_ref[...], sbmT_ref[...]
    for i in range(MLA_HEADS):
        mqT_o[i] = (_rope_rows(rows(qT, i), cmT, samT, sbmT, MLA_ROPE // 4) * mla_qscale).astype(BF16)
    ckvn = _rms(_dot(hb, w_ckv[...])) * g_mkv[...]
    kn = _dot(ckvn.astype(BF16), w_kb[...])
    kr = rope_mla(_dot(hb, w_kr[...]))
    vT = _dot(w_vbT[...], ckvn.T.astype(BF16))
    for i in range(MLA_HEADS):
        sl = slice(i * HEAD_LANES, (i + 1) * HEAD_LANES)
        mk_o[:, sl] = (_head(kn, i) + kr).astype(BF16)
        mvT_o[i, :HEAD_LANES, :] = rows(vT, i).astype(BF16)
        mvT_o[i, HEAD_LANES:, :] = ones_rows

    gqa_qscale = GQA_HD ** -0.5 * LOG2E
    gqT = _dot(w_gqT[...], hTb)
    cT, saT, sbT = cT_ref[...], saT_ref[...], sbT_ref[...]
    for i in range(GQA_HEADS):
        qn = _rms_rows(rows(gqT, i)) * g_gqT[...]
        gqT_o[i] = (_rope_rows(qn, cT, saT, sbT, GQA_HD // 4) * gqa_qscale).astype(BF16)
    gk = _dot(hb, w_gk[...])
    gvT = _dot(w_gvT[...], hTb)
    for i in range(GQA_KV_HEADS):
        sl = slice(i * HEAD_LANES, (i + 1) * HEAD_LANES)
        gk_o[:, sl] = rope_big(_rms(_head(gk, i)) * g_gk[...]).astype(BF16)
        gvT_o[i, :HEAD_LANES, :] = rows(gvT, i).astype(BF16)
        gvT_o[i, HEAD_LANES:, :] = ones_rows


def _inproj_call(xs, mods, gmix, tabs, tabsT, wts, n_lat_tiles, l):
    T, D = xs.shape
    tm = ROW_TILE
    row = lambda w: pl.BlockSpec((tm, w), lambda i: (i, 0))
    col = pl.BlockSpec((HEAD_LANES, tm), lambda i: (0, i))
    headsT = lambda nh, rows=HEAD_LANES: pl.BlockSpec((nh, rows, tm), lambda i: (0, 0, i))
    in_specs = [row(D), _mod_spec(l, n_lat_tiles), _layer_spec(gmix, l)]
    in_specs += [row(HEAD_LANES)] * 6 + [col] * 6 + [_layer_spec(w, l) for w in wts]
    out_shape = (
        jax.ShapeDtypeStruct((T, RET_HEADS * RET_DK), F32),
        jax.ShapeDtypeStruct((T, RET_HEADS * RET_DK), F32),
        jax.ShapeDtypeStruct((T, RET_W), BF16),
        jax.ShapeDtypeStruct((T, RET_W), F32),
        jax.ShapeDtypeStruct((MLA_HEADS, HEAD_LANES, T), BF16),
        jax.ShapeDtypeStruct((T, MLA_HEADS * HEAD_LANES), BF16),
        jax.ShapeDtypeStruct((MLA_HEADS, V_ROWS, T), BF16),
        jax.ShapeDtypeStruct((GQA_HEADS, HEAD_LANES, T), BF16),
        jax.ShapeDtypeStruct((T, GQA_KV_HEADS * GQA_HD), BF16),
        jax.ShapeDtypeStruct((GQA_KV_HEADS, V_ROWS, T), BF16),
    )
    out_specs = (
        row(RET_HEADS * RET_DK), row(RET_HEADS * RET_DK), row(RET_W), row(RET_W),
        headsT(MLA_HEADS), row(MLA_HEADS * HEAD_LANES), headsT(MLA_HEADS, V_ROWS),
        headsT(GQA_HEADS), row(GQA_KV_HEADS * GQA_HD), headsT(GQA_KV_HEADS, V_ROWS),
    )
    return pl.pallas_call(
        _inproj_kernel,
        out_shape=out_shape,
        grid=(T // tm,),
        in_specs=in_specs,
        out_specs=out_specs,
        compiler_params=_params("arbitrary"),
        name="inproj",
    )(xs, mods, gmix, *tabs, *tabsT, *wts)


def _ret_kernel(qf_ref, kf_ref, vf_ref, qb_ref, kb_ref, vb_ref,
                dec_ref, xi_ref, zeta_ref, cd_ref,
                yf_ref, yb_ref, sf_ref, sb_ref):
    @pl.when(pl.program_id(0) == 0)
    def _():
        sf_ref[...] = jnp.zeros_like(sf_ref)
        sb_ref[...] = jnp.zeros_like(sb_ref)

    dirs = ((qf_ref, kf_ref, vf_ref, yf_ref, sf_ref), (qb_ref, kb_ref, vb_ref, yb_ref, sb_ref))
    pending = []
    for d, (q_ref, k_ref, v_ref, y_ref, s_ref) in enumerate(dirs):
        for h in range(RET_HEADS):
            ksl = slice(h * RET_DK, (h + 1) * RET_DK)
            vsl = slice(h * RET_DV, (h + 1) * RET_DV)
            k = k_ref[:, ksl]
            v = v_ref[:, vsl]
            qb16 = q_ref[:, ksl].astype(BF16)
            scores = lax.dot_general(qb16, k.astype(BF16), (((1,), (1,)), ((), ())),
                                     preferred_element_type=F32) * dec_ref[d, h]
            state = s_ref[h]
            cross = _dot(qb16, state.astype(BF16)) * xi_ref[d, h]
            kz = (k * zeta_ref[d, h]).T.astype(BF16)
            s_ref[h] = cd_ref[d, h] * state + _dot(kz, v)
            pending.append((y_ref, vsl, scores.astype(BF16), v, cross))
    for y_ref, vsl, scores, v, cross in pending:
        y_ref[:, vsl] = _dot(scores, v) + cross


def _ret_call(rq, rk, rv, consts, n_lat, n_ctx):
    T = rq.shape[0]
    C = RET_CHUNK
    n = n_lat + n_ctx

    def fwd(i):
        return jnp.where(i < n_ctx, n_lat + i, i - n_ctx)

    def bwd(i):
        return jnp.where(i < n_ctx, n - 1 - i, n - 1 - i)

    kw, vw = RET_HEADS * RET_DK, RET_W
    spec = lambda w, order: pl.BlockSpec((C, w), lambda i: (order(i), 0))
    dec, xi, zeta, cd = consts
    return pl.pallas_call(
        _ret_kernel,
        out_shape=(jax.ShapeDtypeStruct((T, vw), F32), jax.ShapeDtypeStruct((T, vw), F32)),
        grid=(n,),
        in_specs=[spec(kw, fwd), spec(kw, fwd), spec(vw, fwd),
                  spec(kw, bwd), spec(kw, bwd), spec(vw, bwd),
                  _const_spec(dec.shape), _const_spec(xi.shape),
                  _const_spec(zeta.shape), _const_spec(cd.shape)],
        out_specs=(spec(vw, fwd), spec(vw, bwd)),
        scratch_shapes=[pltpu.VMEM((RET_HEADS, RET_DK, RET_DV), F32),
                        pltpu.VMEM((RET_HEADS, RET_DK, RET_DV), F32)],
        compiler_params=_params("arbitrary"),
        name="retention",
    )(rq, rk, rv, rq, rk, rv, dec, xi, zeta, cd)


def _ret_consts():
    C = RET_CHUNK
    h = jnp.arange(RET_HEADS, dtype=F32)
    pos = jnp.arange(C, dtype=F32)
    diff = pos[:, None] - pos[None, :]

    def one(offset, reverse):
        lg = jnp.log1p(-jnp.exp2(-(RET_DECAY_START + offset) - h))
        dd = -diff if reverse else diff
        keep = dd >= 0
        dec = jnp.where(keep[None], jnp.exp(lg[:, None, None] * jnp.where(keep, dd, 0.0)[None]), 0.0)
        p = (C - 1 - pos) if reverse else pos
        xi = jnp.exp(lg[:, None] * (p + 1)[None, :])
        zeta = jnp.exp(lg[:, None] * (C - 1 - p)[None, :])
        cd = jnp.exp(lg * C)
        return dec, xi[:, :, None], zeta[:, :, None], jnp.broadcast_to(cd[:, None, None], (RET_HEADS, 1, 1))

    f = one(0.0, False)
    b = one(RET_BWD_OFFSET, True)
    return tuple(jnp.stack([a, c]) for a, c in zip(f, b))


def _flash_kernel(qT_ref, k_ref, vT_ref, o_ref, m_ref, acc_ref, *buf_refs, tq, nq, tk, nk, lag, unroll, sub):
    n_buf = lag + 1
    s_bufs, c_bufs = buf_refs[:n_buf], buf_refs[n_buf:]
    n_sub = tk // sub

    def key_rows(j, r):
        base = j * tk if isinstance(j, int) else pl.multiple_of(j * tk, tk)
        return pl.ds(base + r * sub, sub)

    def q_tile(i, carry):
        q_cols = pl.ds(i * tq if isinstance(i, int) else pl.multiple_of(i * tq, tq), tq)
        q = qT_ref[0, :, q_cols]
        m_ref[...] = jnp.full_like(m_ref, NEG_BIG)
        acc_ref[...] = jnp.zeros_like(acc_ref)

        def step(j, phase, do_scores, do_values):
            s_cur, cmax_cur = s_bufs[phase], c_bufs[phase]
            s_nxt, cmax_nxt = s_bufs[(phase + lag) % n_buf], c_bufs[(phase + lag) % n_buf]
            if do_values:
                m_prev = m_ref[...]
                m_new = jnp.maximum(m_prev, cmax_cur[...])
                alpha = jnp.exp2(m_prev - m_new)
                m_ref[...] = m_new
            cmax = pv = None
            for r in range(n_sub):
                rows = pl.ds(r * sub, sub)
                if do_values:
                    p = jnp.exp2(s_cur[rows, :] - m_new).astype(BF16)
                    part = _dot(vT_ref[0, :, key_rows(j, r)], p)
                    pv = part if pv is None else pv + part
                if do_scores:
                    s = _dot(k_ref[key_rows(j + lag, r), :], q)
                    s_nxt[rows, :] = s
                    part = jnp.max(s, axis=0, keepdims=True)
                    cmax = part if cmax is None else jnp.maximum(cmax, part)
            if do_scores:
                cmax_nxt[...] = cmax
            if do_values:
                acc_ref[...] = alpha * acc_ref[...] + pv

        def body(i, carry):
            for u in range(unroll):
                step(i * unroll + u, u % n_buf, True, True)
            return carry

        for j in range(-lag, 0):
            step(j, j % n_buf, j + lag < nk, False)
        n_body = max(nk - lag, 0) // unroll
        if n_body < 2:
            n_body = 0
        if n_body:
            lax.fori_loop(0, n_body, body, 0)
        for j in range(n_body * unroll, nk):
            step(j, j % n_buf, j + lag < nk, True)
        acc = acc_ref[...]
        o_ref[q_cols, :] = (acc[:HEAD_LANES] / acc[HEAD_LANES:HEAD_LANES + 1]).T.astype(o_ref.dtype)
        return carry

    if nq == 1:
        q_tile(0, 0)
    else:
        lax.fori_loop(0, nq, q_tile, 0)


def _flash_kernel_into(qT_ref, k_ref, vT_ref, y_hbm_ref, o_ref, *scratch, **kw):
    del y_hbm_ref
    _flash_kernel(qT_ref, k_ref, vT_ref, o_ref, *scratch, **kw)


def _flash_call(qT, k, vT, y_prev, *, group, q_rows, q_off, kv_rows, kv_off, tq, tk, name):
    hq, T = qT.shape[0], qT.shape[2]
    sub = min(tk, FLASH_SUB)
    n_buf = FLASH_LAG + 1
    assert q_rows % tq == 0 and q_off % q_rows == 0 and kv_rows % tk == 0 and tk % sub == 0
    assert FLASH_UNROLL % n_buf == 0
    nq, nk = q_rows // tq, kv_rows // tk
    qb, kb = q_off // q_rows, kv_off // kv_rows
    kw = dict(tq=tq, nq=nq, tk=tk, nk=nk, lag=FLASH_LAG, unroll=FLASH_UNROLL, sub=sub)
    in_specs = [
        pl.BlockSpec((1, HEAD_LANES, q_rows), lambda h: (h, 0, qb)),
        pl.BlockSpec((kv_rows, HEAD_LANES), lambda h: (kb, h // group)),
        pl.BlockSpec((1, V_ROWS, kv_rows), lambda h: (h // group, 0, kb)),
    ]
    operands = (qT, k, vT)
    if y_prev is not None:
        in_specs.append(pl.BlockSpec(memory_space=pl.ANY))
        operands += (y_prev,)
    return pl.pallas_call(
        functools.partial(_flash_kernel if y_prev is None else _flash_kernel_into, **kw),
        out_shape=jax.ShapeDtypeStruct((T, hq * HEAD_LANES), BF16),
        grid=(hq,),
        in_specs=in_specs,
        out_specs=pl.BlockSpec((q_rows, HEAD_LANES), lambda h: (qb, h)),
        scratch_shapes=[pltpu.VMEM((1, tq), F32), pltpu.VMEM((V_ROWS, tq), F32)]
                       + [pltpu.VMEM((tk, tq), F32)] * n_buf + [pltpu.VMEM((1, tq), F32)] * n_buf,
        input_output_aliases={} if y_prev is None else {3: 0},
        compiler_params=_params("arbitrary"),
        name=name,
    )(*operands)


def _pick_tile(n, candidates):
    for t in candidates:
        if n % t == 0:
            return t
    raise ValueError(f"no tile for {n}")


def _attention(qT, k, vT, group, L, Cn, name):
    T = L + Cn
    tq = _pick_tile(L, (1024, 512, 256, 128))
    tk = _pick_tile(T, (1280, 256, 128))
    y = _flash_call(qT, k, vT, None, group=group, q_rows=L, q_off=0, kv_rows=T, kv_off=0,
                    tq=tq, tk=tk, name=name)
    return _flash_call(qT, k, vT, y, group=group, q_rows=Cn, q_off=L, kv_rows=Cn, kv_off=L,
                       tq=Cn, tk=Cn, name=name + "_ctx")


def _merge_kernel(x_ref, mod_ref, gmix_ref, yf_ref, yb_ref, rg_ref, ym_ref, yg_ref,
                  w_gates, w_ro, w_mo, w_go, w_out, o_ref):
    x = x_ref[...]
    mod = mod_ref[0]
    hb = _prenorm(x, mod, gmix_ref[...], 0, 1).astype(BF16)
    gs = jax.nn.sigmoid(_dot(hb, w_gates[...]))
    y = yf_ref[...] + yb_ref[...]
    g = _silu(rg_ref[...])
    parts = []
    for h in range(RET_HEADS):
        sl = slice(h * RET_DV, (h + 1) * RET_DV)
        parts.append((g[:, sl] * _rms(y[:, sl])).astype(BF16))
    y_ret = jnp.concatenate(parts, axis=-1)
    D = D_MODEL
    z = (gs[:, :D] * _dot(y_ret, w_ro[...])
         + gs[:, D:2 * D] * _dot(ym_ref[...], w_mo[...])
         + gs[:, 2 * D:] * _dot(yg_ref[...], w_go[...]))
    o_ref[...] = x + mod[2:3] * _dot(z.astype(BF16), w_out[...])


def _merge_call(xs, mods, gmix, yf, yb, rg, ym, yg, wts, n_lat_tiles, l):
    T, D = xs.shape
    tm = ROW_TILE
    row = lambda w: pl.BlockSpec((tm, w), lambda i: (i, 0))
    return pl.pallas_call(
        _merge_kernel,
        out_shape=jax.ShapeDtypeStruct((T, D), F32),
        grid=(T // tm,),
        in_specs=[row(D), _mod_spec(l, n_lat_tiles), _layer_spec(gmix, l),
                  row(RET_W), row(RET_W), row(RET_W), row(MLA_W), row(GQA_W)]
                 + [_layer_spec(w, l) for w in wts],
        out_specs=row(D),
        compiler_params=_params("arbitrary"),
        name="merge",
    )(xs, mods, gmix, yf, yb, rg, ym, yg, *wts)


def _ffn_kernel(x_ref, mod_ref, g_ref, w_a, w_b, w_o, gf_ref, o_ref, *, final_norm):
    x = x_ref[...]
    mod = mod_ref[0]
    hb = _prenorm(x, mod, g_ref[...], 3, 4).astype(BF16)
    u = (_silu(_dot(hb, w_a[...])) * _dot(hb, w_b[...])).astype(BF16)
    y = x + mod[5:6] * _dot(u, w_o[...])
    o_ref[...] = _rms(y) * gf_ref[...] if final_norm else y


def _ffn_call(xs, mods, g, w_a, w_b, w_o, g_final, n_lat_tiles, l, rows, final_norm):
    D = xs.shape[1]
    tm = ROW_TILE
    row = pl.BlockSpec((tm, D), lambda i: (i, 0))
    return pl.pallas_call(
        functools.partial(_ffn_kernel, final_norm=final_norm),
        out_shape=jax.ShapeDtypeStruct((rows, D), F32),
        grid=(rows // tm,),
        in_specs=[row, _mod_spec(l, n_lat_tiles), _layer_spec(g, l), _layer_spec(w_a, l),
                  _layer_spec(w_b, l), _layer_spec(w_o, l), _const_spec((1, D))],
        out_specs=row,
        compiler_params=_params("arbitrary"),
        name="ffn",
    )(xs, mods, g, w_a, w_b, w_o, g_final)


def _rope_tables(L, Cn):
    n_rows = L // GRID_W

    def angles(n, dim):
        half = dim // 2
        inv_freq = ROPE_BASE ** (-jnp.arange(half, dtype=F32) / half)
        ang = jnp.arange(n, dtype=jnp.int32).astype(F32)[:, None] * inv_freq[None, :]
        return jnp.cos(ang), jnp.sin(ang)

    def tables(dim, lane0):
        cr, sr = (jnp.repeat(a, GRID_W, axis=0) for a in angles(n_rows, dim // 2))
        cc, sc = (jnp.tile(a, (n_rows, 1)) for a in angles(GRID_W, dim // 2))
        z = jnp.zeros_like(sr)
        c = jnp.concatenate([cr, cr, cc, cc], axis=1)
        sa = jnp.concatenate([-sr, z, -sc, z], axis=1)
        sb = jnp.concatenate([z, sr, z, sc], axis=1)
        pad = lambda a, fill: jnp.concatenate(
            [jnp.full((L, lane0), fill, F32), a, jnp.full((L, HEAD_LANES - lane0 - dim), fill, F32)], axis=1)
        c, sa, sb = pad(c, 1.0), pad(sa, 0.0), pad(sb, 0.0)
        ctx = lambda fill: jnp.full((Cn, HEAD_LANES), fill, F32)
        return (jnp.concatenate([c, ctx(1.0)]), jnp.concatenate([sa, ctx(0.0)]),
                jnp.concatenate([sb, ctx(0.0)]))

    return tables(RET_DK, 0) + tables(MLA_ROPE, MLA_NOPE)


def _stacked_weights(w_in, w_mla_qb, w_mla_kvb):
    depth = w_in.shape[0]
    split_at = np.cumsum(IN_SIZES)[:-1].tolist()
    rq, rk, rv, rg, cq, ckv, kr, gq, gk, gv, gates = jnp.split(w_in.astype(BF16), split_at, axis=-1)
    kr_pad = jnp.pad(kr, ((0, 0), (0, 0), (MLA_NOPE, HEAD_LANES - MLA_NOPE - MLA_ROPE)))
    qb = w_mla_qb.astype(BF16).reshape(depth, MLA_Q_RANK, MLA_HEADS, MLA_NOPE + MLA_ROPE)
    qb = jnp.pad(qb, ((0, 0), (0, 0), (0, 0), (0, HEAD_LANES - MLA_NOPE - MLA_ROPE)))
    kvb = w_mla_kvb.astype(BF16).reshape(depth, MLA_KV_RANK, MLA_HEADS, MLA_NOPE + MLA_V)
    kb = jnp.pad(kvb[..., :MLA_NOPE], ((0, 0), (0, 0), (0, 0), (0, HEAD_LANES - MLA_NOPE)))
    flat = lambda w: w.reshape(depth, w.shape[1], -1)
    wT = lambda w: jnp.swapaxes(w, 1, 2)
    natural = (rq, rk, rv, rg, ckv, kr_pad, gk, flat(kb))
    transposed = (wT(cq), wT(flat(qb)), wT(flat(kvb[..., MLA_NOPE:])), wT(gq), wT(gv))
    return natural, transposed, gates


def kernel(x, c, ctx, c_ctx, w_mod, b_mod, g_mix, w_in, g_mla_q, g_mla_kv, w_mla_qb, w_mla_kvb,
           g_gqa_q, g_gqa_k, w_ret_o, w_mla_o, w_gqa_o, w_out, g_ffn, w_ffn_in, w_ffn_out, g_final):
    B, L, D = x.shape
    Cn = ctx.shape[1]
    depth = w_mod.shape[0]
    assert B == 1 and D == D_MODEL and L % ROW_TILE == 0 and Cn % ROW_TILE == 0 and L % Cn == 0
    assert L % GRID_W == 0
    n_lat_tiles = L // ROW_TILE
    ffn_hidden = w_ffn_out.shape[1]

    xs = jnp.concatenate([x[0], ctx[0]], axis=0)
    cond = jnp.zeros((MOD_ROWS, D), F32).at[0].set(c[0]).at[1].set(c_ctx)
    mods = _mod_call(cond, w_mod, b_mod)
    mods = mods[:, :2].reshape(depth, 2, N_MOD, D)
    mods = jnp.pad(mods, ((0, 0), (0, 0), (0, MOD_ROWS - N_MOD), (0, 0)))

    tabs = _rope_tables(L, Cn)
    ret_consts = _ret_consts()
    rows3 = lambda g: g.reshape(depth, 1, -1)
    w_nat, w_tr, w_gates = _stacked_weights(w_in, w_mla_qb, w_mla_kvb)
    lanes = lambda g: jnp.broadcast_to(g[:, :, None], g.shape + (ROW_TILE,))
    wts = w_nat + w_tr + (rows3(g_mla_kv), rows3(g_gqa_k), lanes(g_mla_q), lanes(g_gqa_q))
    tabsT = tuple(t.T for t in tabs)
    merge_w = (w_gates,) + tuple(w.astype(BF16) for w in (w_ret_o, w_mla_o, w_gqa_o, w_out))
    w_ffn = w_ffn_in.astype(BF16)
    w_ffn_a, w_ffn_b, w_ffn_o = w_ffn[..., :ffn_hidden], w_ffn[..., ffn_hidden:], w_ffn_out.astype(BF16)
    g_mix3, g_ffn3 = rows3(g_mix), rows3(g_ffn)

    for l in range(depth):
        rq, rk, rv, rg, mqT, mk, mvT, gqT, gk, gvT = _inproj_call(
            xs, mods, g_mix3, tabs, tabsT, wts, n_lat_tiles, l)
        yf, yb = _ret_call(rq, rk, rv, ret_consts, L // RET_CHUNK, Cn // RET_CHUNK)
        ym = _attention(mqT, mk, mvT, 1, L, Cn, "mla")
        yg = _attention(gqT, gk, gvT, GQA_HEADS // GQA_KV_HEADS, L, Cn, "gqa")
        xs = _merge_call(xs, mods, g_mix3, yf, yb, rg, ym, yg, merge_w, n_lat_tiles, l)
        last = l == depth - 1
        xs = _ffn_call(xs, mods, g_ffn3, w_ffn_a, w_ffn_b, w_ffn_o, g_final.reshape(1, -1), n_lat_tiles, l,
                       rows=L if last else L + Cn, final_norm=last)
    return xs[None]
```

```python
import functools
import math

import jax
import jax.numpy as jnp
import numpy as np
from jax import lax
from jax.experimental import pallas as pl
from jax.experimental.pallas import tpu as pltpu

D_MODEL = 1024
GRID_W = 64
RET_CHUNK = 128
ROPE_BASE = 10000.0
NORM_EPS = 1e-6
N_MOD = 6

RET_HEADS = 4
RET_DK = 128
RET_DV = 256
RET_DECAY_START = 5.0
RET_BWD_OFFSET = 0.5

MLA_HEADS = 8
MLA_Q_RANK = 256
MLA_KV_RANK = 256
MLA_NOPE = 64
MLA_ROPE = 32
MLA_V = 128

GQA_HEADS = 8
GQA_KV_HEADS = 2
GQA_HD = 128

RET_W = RET_HEADS * RET_DV
MLA_W = MLA_HEADS * MLA_V
GQA_W = GQA_HEADS * GQA_HD
N_BRANCH = 3

IN_SIZES = (
    RET_HEADS * RET_DK, RET_HEADS * RET_DK, RET_W, RET_W,
    MLA_Q_RANK, MLA_KV_RANK, MLA_ROPE,
    GQA_W, GQA_KV_HEADS * GQA_HD, GQA_KV_HEADS * GQA_HD,
    N_BRANCH * D_MODEL,
)

HEAD_LANES = 128
BF16_SUBLANES = 16
V_ROWS = HEAD_LANES + BF16_SUBLANES
MOD_ROWS = 8
ROW_TILE = 256
FLASH_LAG = 2
FLASH_UNROLL = 3
FLASH_SUB = 256
V7X_VMEM_LIMIT = 56 * 1024 * 1024
LOG2E = math.log2(math.e)
NEG_BIG = -1e30

F32 = jnp.float32
BF16 = jnp.bfloat16


def _params(*sem, flags=None):
    return pltpu.CompilerParams(dimension_semantics=sem, vmem_limit_bytes=V7X_VMEM_LIMIT, flags=flags)


def _const_spec(shape):
    zeros = (0,) * len(shape)
    return pl.BlockSpec(shape, lambda *_: zeros)


def _layer_spec(w, l, pipeline_mode=None):
    zeros = (0,) * (w.ndim - 1)
    return pl.BlockSpec((None,) + w.shape[1:], lambda *_: (l,) + zeros, pipeline_mode=pipeline_mode)


def _mod_spec(l, n_lat_tiles):
    return pl.BlockSpec((None, 1, MOD_ROWS, D_MODEL), lambda i: (l, i // n_lat_tiles, 0, 0))


def _dot(a, b):
    return jnp.dot(a, b, preferred_element_type=F32)


def _rms(x):
    return x * lax.rsqrt(jnp.mean(x * x, axis=-1, keepdims=True) + NORM_EPS)


def _silu(x):
    return x * jax.nn.sigmoid(x)


def _rope(x, c, sa, sb, block):
    up = pltpu.roll(x, HEAD_LANES - block, 1)
    dn = pltpu.roll(x, block, 1)
    return x * c + up * sa + dn * sb


def _head(x, h):
    return x[:, h * HEAD_LANES:(h + 1) * HEAD_LANES]


def _mod_kernel(cond_ref, w_ref, b_ref, o_ref):
    s = _silu(cond_ref[...]).astype(BF16)
    o_ref[0] = _dot(s, w_ref[0].astype(BF16)) + b_ref[0]


def _mod_call(cond, w_mod, b_mod):
    depth, d, n = w_mod.shape
    tn = n // 4
    return pl.pallas_call(
        _mod_kernel,
        out_shape=jax.ShapeDtypeStruct((depth, MOD_ROWS, n), F32),
        grid=(depth, n // tn),
        in_specs=[
            pl.BlockSpec((MOD_ROWS, d), lambda l, j: (0, 0)),
            pl.BlockSpec((1, d, tn), lambda l, j: (l, 0, j)),
            pl.BlockSpec((1, 1, tn), lambda l, j: (l, 0, j)),
        ],
        out_specs=pl.BlockSpec((1, MOD_ROWS, tn), lambda l, j: (l, 0, j)),
        compiler_params=_params("arbitrary", "arbitrary"),
        name="mod",
    )(cond, w_mod, b_mod.reshape(depth, 1, n))


def _prenorm(x, mod, g, shift_row, scale_row):
    h = _rms(x) * g
    return h * (1.0 + mod[scale_row:scale_row + 1]) + mod[shift_row:shift_row + 1]


def _rms_rows(x):
    return x * lax.rsqrt(jnp.mean(x * x, axis=0, keepdims=True) + NORM_EPS)


def _rope_rows(x, c, sa, sb, block):
    up = jnp.concatenate([x[block:], x[:block]], axis=0)
    dn = jnp.concatenate([x[-block:], x[:-block]], axis=0)
    return x * c + up * sa + dn * sb


def _inproj_kernel(x_ref, mod_ref, gmix_ref, c_ref, sa_ref, sb_ref, cm_ref, sam_ref, sbm_ref,
                   cT_ref, saT_ref, sbT_ref, cmT_ref, samT_ref, sbmT_ref,
                   w_rq, w_rk, w_rv, w_rg, w_ckv, w_kr, w_gk, w_kb,
                   w_cqT, w_qbT, w_vbT, w_gqT, w_gvT, g_mkv, g_gk, g_mqT, g_gqT,
                   rq_o, rk_o, rv_o, rg_o, mqT_o, mk_o, mvT_o, gqT_o, gk_o, gvT_o):
    h = _prenorm(x_ref[...], mod_ref[0], gmix_ref[...], 0, 1)
    hb = h.astype(BF16)
    hTb = h.T.astype(BF16)
    c, sa, sb = c_ref[...], sa_ref[...], sb_ref[...]
    cm, sam, sbm = cm_ref[...], sam_ref[...], sbm_ref[...]
    rope_big = lambda v: _rope(v, c, sa, sb, RET_DK // 4)
    rope_mla = lambda v: _rope(v, cm, sam, sbm, MLA_ROPE // 4)
    rows = lambda v, i: v[i * HEAD_LANES:(i + 1) * HEAD_LANES]
    ones_rows = (lax.broadcasted_iota(jnp.int32, (BF16_SUBLANES, x_ref.shape[0]), 0) == 0).astype(BF16)

    cqnT = (_rms_rows(_dot(w_cqT[...], hTb)) * g_mqT[...]).astype(BF16)
    ckvn = _rms(_dot(hb, w_ckv[...])) * g_mkv[...]

    rq = _dot(hb, w_rq[...])
    rk = _dot(hb, w_rk[...])
    for i in range(RET_HEADS):
        sl = slice(i * HEAD_LANES, (i + 1) * HEAD_LANES)
        rq_o[:, sl] = rope_big(_head(rq, i))
        rk_o[:, sl] = rope_big(_head(rk, i)) * (RET_DK ** -0.5)

    gqa_qscale = GQA_HD ** -0.5 * LOG2E
    gqT = _dot(w_gqT[...], hTb)
    cT, saT, sbT = cT_ref[...], saT_ref[...], sbT_ref[...]
    for i in range(GQA_HEADS):
        qn = _rms_rows(rows(gqT, i)) * g_gqT[...]
        gqT_o[i] = (_rope_rows(qn, cT, saT, sbT, GQA_HD // 4) * gqa_qscale).astype(BF16)
    gk = _dot(hb, w_gk[...])
    gvT = _dot(w_gvT[...], hTb)
    for i in range(GQA_KV_HEADS):
        sl = slice(i * HEAD_LANES, (i + 1) * HEAD_LANES)
        gk_o[:, sl] = rope_big(_rms(_head(gk, i)) * g_gk[...]).astype(BF16)
        gvT_o[i, :HEAD_LANES, :] = rows(gvT, i).astype(BF16)
        gvT_o[i, HEAD_LANES:, :] = ones_rows

    rv_o[...] = _dot(hb, w_rv[...]).astype(BF16)
    rg_o[...] = _dot(hb, w_rg[...])

    mla_qscale = (MLA_NOPE + MLA_ROPE) ** -0.5 * LOG2E
    kr = rope_mla(_dot(hb, w_kr[...]))
    qT = _dot(w_qbT[...], cqnT)
    kn = _dot(ckvn.astype(BF16), w_kb[...])
    vT = _dot(w_vbT[...], ckvn.T.astype(BF16))
    cmT, samT, sbmT = cmT_ref[...], samT_ref[...], sbmT_ref[...]
    for i in range(MLA_HEADS):
        sl = slice(i * HEAD_LANES, (i + 1) * HEAD_LANES)
        mqT_o[i] = (_rope_rows(rows(qT, i), cmT, samT, sbmT, MLA_ROPE // 4) * mla_qscale).astype(BF16)
        mk_o[:, sl] = (_head(kn, i) + kr).astype(BF16)
        mvT_o[i, :HEAD_LANES, :] = rows(vT, i).astype(BF16)
        mvT_o[i, HEAD_LANES:, :] = ones_rows


def _inproj_call(xs, mods, gmix, tabs, tabsT, wts, n_lat_tiles, l):
    T, D = xs.shape
    tm = ROW_TILE
    row = lambda w: pl.BlockSpec((tm, w), lambda i: (i, 0))
    col = pl.BlockSpec((HEAD_LANES, tm), lambda i: (0, i))
    headsT = lambda nh, rows=HEAD_LANES: pl.BlockSpec((nh, rows, tm), lambda i: (0, 0, i))
    in_specs = [row(D), _mod_spec(l, n_lat_tiles), _layer_spec(gmix, l)]
    in_specs += [row(HEAD_LANES)] * 6 + [col] * 6 + [_layer_spec(w, l) for w in wts]
    out_shape = (
        jax.ShapeDtypeStruct((T, RET_HEADS * RET_DK), F32),
        jax.ShapeDtypeStruct((T, RET_HEADS * RET_DK), F32),
        jax.ShapeDtypeStruct((T, RET_W), BF16),
        jax.ShapeDtypeStruct((T, RET_W), F32),
        jax.ShapeDtypeStruct((MLA_HEADS, HEAD_LANES, T), BF16),
        jax.ShapeDtypeStruct((T, MLA_HEADS * HEAD_LANES), BF16),
        jax.ShapeDtypeStruct((MLA_HEADS, V_ROWS, T), BF16),
        jax.ShapeDtypeStruct((GQA_HEADS, HEAD_LANES, T), BF16),
        jax.ShapeDtypeStruct((T, GQA_KV_HEADS * GQA_HD), BF16),
        jax.ShapeDtypeStruct((GQA_KV_HEADS, V_ROWS, T), BF16),
    )
    out_specs = (
        row(RET_HEADS * RET_DK), row(RET_HEADS * RET_DK), row(RET_W), row(RET_W),
        headsT(MLA_HEADS), row(MLA_HEADS * HEAD_LANES), headsT(MLA_HEADS, V_ROWS),
        headsT(GQA_HEADS), row(GQA_KV_HEADS * GQA_HD), headsT(GQA_KV_HEADS, V_ROWS),
    )
    return pl.pallas_call(
        _inproj_kernel,
        out_shape=out_shape,
        grid=(T // tm,),
        in_specs=in_specs,
        out_specs=out_specs,
        compiler_params=_params("arbitrary"),
        name="inproj",
    )(xs, mods, gmix, *tabs, *tabsT, *wts)


def _ret_kernel(qf_ref, kf_ref, vf_ref, qb_ref, kb_ref, vb_ref,
                dec_ref, xi_ref, zeta_ref, cd_ref,
                yf_ref, yb_ref, sf_ref, sb_ref):
    @pl.when(pl.program_id(0) == 0)
    def _():
        sf_ref[...] = jnp.zeros_like(sf_ref)
        sb_ref[...] = jnp.zeros_like(sb_ref)

    dirs = ((qf_ref, kf_ref, vf_ref, yf_ref, sf_ref), (qb_ref, kb_ref, vb_ref, yb_ref, sb_ref))
    pending = []
    for d, (q_ref, k_ref, v_ref, y_ref, s_ref) in enumerate(dirs):
        for h in range(RET_HEADS):
            ksl = slice(h * RET_DK, (h + 1) * RET_DK)
            vsl = slice(h * RET_DV, (h + 1) * RET_DV)
            k = k_ref[:, ksl]
            v = v_ref[:, vsl]
            qb16 = q_ref[:, ksl].astype(BF16)
            scores = lax.dot_general(qb16, k.astype(BF16), (((1,), (1,)), ((), ())),
                                     preferred_element_type=F32) * dec_ref[d, h]
            state = s_ref[h]
            cross = _dot(qb16, state.astype(BF16)) * xi_ref[d, h]
            kz = (k * zeta_ref[d, h]).T.astype(BF16)
            s_ref[h] = cd_ref[d, h] * state + _dot(kz, v)
            pending.append((y_ref, vsl, scores.astype(BF16), v, cross))
    for y_ref, vsl, scores, v, cross in pending:
        y_ref[:, vsl] = _dot(scores, v) + cross


def _ret_call(rq, rk, rv, consts, n_lat, n_ctx):
    T = rq.shape[0]
    C = RET_CHUNK
    n = n_lat + n_ctx

    def fwd(i):
        return jnp.where(i < n_ctx, n_lat + i, i - n_ctx)

    def bwd(i):
        return jnp.where(i < n_ctx, n - 1 - i, n - 1 - i)

    kw, vw = RET_HEADS * RET_DK, RET_W
    spec = lambda w, order: pl.BlockSpec((C, w), lambda i: (order(i), 0))
    dec, xi, zeta, cd = consts
    return pl.pallas_call(
        _ret_kernel,
        out_shape=(jax.ShapeDtypeStruct((T, vw), F32), jax.ShapeDtypeStruct((T, vw), F32)),
        grid=(n,),
        in_specs=[spec(kw, fwd), spec(kw, fwd), spec(vw, fwd),
                  spec(kw, bwd), spec(kw, bwd), spec(vw, bwd),
                  _const_spec(dec.shape), _const_spec(xi.shape),
                  _const_spec(zeta.shape), _const_spec(cd.shape)],
        out_specs=(spec(vw, fwd), spec(vw, bwd)),
        scratch_shapes=[pltpu.VMEM((RET_HEADS, RET_DK, RET_DV), F32),
                        pltpu.VMEM((RET_HEADS, RET_DK, RET_DV), F32)],
        compiler_params=_params("arbitrary"),
        name="retention",
    )(rq, rk, rv, rq, rk, rv, dec, xi, zeta, cd)


def _ret_consts():
    C = RET_CHUNK
    h = jnp.arange(RET_HEADS, dtype=F32)
    pos = jnp.arange(C, dtype=F32)
    diff = pos[:, None] - pos[None, :]

    def one(offset, reverse):
        lg = jnp.log1p(-jnp.exp2(-(RET_DECAY_START + offset) - h))
        dd = -diff if reverse else diff
        keep = dd >= 0
        dec = jnp.where(keep[None], jnp.exp(lg[:, None, None] * jnp.where(keep, dd, 0.0)[None]), 0.0)
        p = (C - 1 - pos) if reverse else pos
        xi = jnp.exp(lg[:, None] * (p + 1)[None, :])
        zeta = jnp.exp(lg[:, None] * (C - 1 - p)[None, :])
        cd = jnp.exp(lg * C)
        return dec, xi[:, :, None], zeta[:, :, None], jnp.broadcast_to(cd[:, None, None], (RET_HEADS, 1, 1))

    f = one(0.0, False)
    b = one(RET_BWD_OFFSET, True)
    return tuple(jnp.stack([a, c]) for a, c in zip(f, b))


def _flash_kernel(qT_ref, k_ref, vT_ref, o_ref, m_ref, acc_ref, *buf_refs, tq, nq, tk, nk, lag, unroll, sub):
    n_buf = lag + 1
    s_bufs, c_bufs = buf_refs[:n_buf], buf_refs[n_buf:]
    n_sub = tk // sub

    def key_rows(j, r):
        base = j * tk if isinstance(j, int) else pl.multiple_of(j * tk, tk)
        return pl.ds(base + r * sub, sub)

    def q_tile(i, carry):
        q_cols = pl.ds(i * tq if isinstance(i, int) else pl.multiple_of(i * tq, tq), tq)
        q = qT_ref[0, :, q_cols]
        m_ref[...] = jnp.full_like(m_ref, NEG_BIG)
        acc_ref[...] = jnp.zeros_like(acc_ref)

        def step(j, phase, do_scores, do_values):
            s_cur, cmax_cur = s_bufs[phase], c_bufs[phase]
            s_nxt, cmax_nxt = s_bufs[(phase + lag) % n_buf], c_bufs[(phase + lag) % n_buf]
            if do_values:
                m_prev = m_ref[...]
                m_new = jnp.maximum(m_prev, cmax_cur[...])
                alpha = jnp.exp2(m_prev - m_new)
                m_ref[...] = m_new
            cmax = pv = None
            for r in range(n_sub):
                rows = pl.ds(r * sub, sub)
                if do_values:
                    p = jnp.exp2(s_cur[rows, :] - m_new).astype(BF16)
                    part = _dot(vT_ref[0, :, key_rows(j, r)], p)
                    pv = part if pv is None else pv + part
                if do_scores:
                    s = _dot(k_ref[key_rows(j + lag, r), :], q)
                    s_nxt[rows, :] = s
                    part = jnp.max(s, axis=0, keepdims=True)
                    cmax = part if cmax is None else jnp.maximum(cmax, part)
            if do_scores:
                cmax_nxt[...] = cmax
            if do_values:
                acc_ref[...] = alpha * acc_ref[...] + pv

        def body(i, carry):
            for u in range(unroll):
                step(i * unroll + u, u % n_buf, True, True)
            return carry

        for j in range(-lag, 0):
            step(j, j % n_buf, j + lag < nk, False)
        n_body = max(nk - lag, 0) // unroll
        if n_body < 2:
            n_body = 0
        if n_body:
            lax.fori_loop(0, n_body, body, 0)
        for j in range(n_body * unroll, nk):
            step(j, j % n_buf, j + lag < nk, True)
        acc = acc_ref[...]
        o_ref[q_cols, :] = (acc[:HEAD_LANES] / acc[HEAD_LANES:HEAD_LANES + 1]).T.astype(o_ref.dtype)
        return carry

    if nq == 1:
        q_tile(0, 0)
    else:
        lax.fori_loop(0, nq, q_tile, 0)


def _flash_kernel_into(qT_ref, k_ref, vT_ref, y_hbm_ref, o_ref, *scratch, **kw):
    del y_hbm_ref
    _flash_kernel(qT_ref, k_ref, vT_ref, o_ref, *scratch, **kw)


def _flash_call(qT, k, vT, y_prev, *, group, q_rows, q_off, kv_rows, kv_off, tq, tk, name):
    hq, T = qT.shape[0], qT.shape[2]
    sub = min(tk, FLASH_SUB)
    n_buf = FLASH_LAG + 1
    assert q_rows % tq == 0 and q_off % q_rows == 0 and kv_rows % tk == 0 and tk % sub == 0
    assert FLASH_UNROLL % n_buf == 0
    nq, nk = q_rows // tq, kv_rows // tk
    qb, kb = q_off // q_rows, kv_off // kv_rows
    kw = dict(tq=tq, nq=nq, tk=tk, nk=nk, lag=FLASH_LAG, unroll=FLASH_UNROLL, sub=sub)
    in_specs = [
        pl.BlockSpec((1, HEAD_LANES, q_rows), lambda h: (h, 0, qb)),
        pl.BlockSpec((kv_rows, HEAD_LANES), lambda h: (kb, h // group)),
        pl.BlockSpec((1, V_ROWS, kv_rows), lambda h: (h // group, 0, kb)),
    ]
    operands = (qT, k, vT)
    if y_prev is not None:
        in_specs.append(pl.BlockSpec(memory_space=pl.ANY))
        operands += (y_prev,)
    return pl.pallas_call(
        functools.partial(_flash_kernel if y_prev is None else _flash_kernel_into, **kw),
        out_shape=jax.ShapeDtypeStruct((T, hq * HEAD_LANES), BF16),
        grid=(hq,),
        in_specs=in_specs,
        out_specs=pl.BlockSpec((q_rows, HEAD_LANES), lambda h: (qb, h)),
        scratch_shapes=[pltpu.VMEM((1, tq), F32), pltpu.VMEM((V_ROWS, tq), F32)]
                       + [pltpu.VMEM((tk, tq), F32)] * n_buf + [pltpu.VMEM((1, tq), F32)] * n_buf,
        input_output_aliases={} if y_prev is None else {3: 0},
        compiler_params=_params("arbitrary"),
        name=name,
    )(*operands)


def _pick_tile(n, candidates):
    for t in candidates:
        if n % t == 0:
            return t
    raise ValueError(f"no tile for {n}")


def _attention(qT, k, vT, group, L, Cn, name):
    T = L + Cn
    tq = _pick_tile(L, (1024, 512, 256, 128))
    tk = _pick_tile(T, (1280, 256, 128))
    y = _flash_call(qT, k, vT, None, group=group, q_rows=L, q_off=0, kv_rows=T, kv_off=0,
                    tq=tq, tk=tk, name=name)
    return _flash_call(qT, k, vT, y, group=group, q_rows=Cn, q_off=L, kv_rows=Cn, kv_off=L,
                       tq=Cn, tk=Cn, name=name + "_ctx")


def _mix_ffn_kernel(x_ref, mod_ref, gmix_ref, yf_ref, yb_ref, rg_ref, ym_ref, yg_ref,
                    w_gates, w_ro, w_mo, w_go, w_out, gffn_ref, w_a, w_b, w_o, gf_ref, o_ref,
                    *, final_norm):
    x = x_ref[...]
    mod = mod_ref[0]
    hb = _prenorm(x, mod, gmix_ref[...], 0, 1).astype(BF16)
    gs = jax.nn.sigmoid(_dot(hb, w_gates[...]))
    y = yf_ref[...] + yb_ref[...]
    g = _silu(rg_ref[...])
    parts = []
    for h in range(RET_HEADS):
        sl = slice(h * RET_DV, (h + 1) * RET_DV)
        parts.append((g[:, sl] * _rms(y[:, sl])).astype(BF16))
    y_ret = jnp.concatenate(parts, axis=-1)
    D = D_MODEL
    z = (gs[:, :D] * _dot(y_ret, w_ro[...])
         + gs[:, D:2 * D] * _dot(ym_ref[...], w_mo[...])
         + gs[:, 2 * D:] * _dot(yg_ref[...], w_go[...]))
    x = x + mod[2:3] * _dot(z.astype(BF16), w_out[...])

    hb = _prenorm(x, mod, gffn_ref[...], 3, 4).astype(BF16)
    u = (_silu(_dot(hb, w_a[...])) * _dot(hb, w_b[...])).astype(BF16)
    x = x + mod[5:6] * _dot(u, w_o[...])
    o_ref[...] = _rms(x) * gf_ref[...] if final_norm else x


def _mix_ffn_call(xs, mods, gmix, yf, yb, rg, ym, yg, merge_w, g_ffn, ffn_w, g_final, n_lat_tiles, l,
                  rows, final_norm):
    D = xs.shape[1]
    tm = ROW_TILE
    row = lambda w: pl.BlockSpec((tm, w), lambda i: (i, 0))
    once = lambda w: _layer_spec(w, l, pl.Buffered(1))
    return pl.pallas_call(
        functools.partial(_mix_ffn_kernel, final_norm=final_norm),
        out_shape=jax.ShapeDtypeStruct((rows, D), F32),
        grid=(rows // tm,),
        in_specs=[row(D), _mod_spec(l, n_lat_tiles), _layer_spec(gmix, l),
                  row(RET_W), row(RET_W), row(RET_W), row(MLA_W), row(GQA_W)]
                 + [once(w) for w in merge_w] + [_layer_spec(g_ffn, l)] + [once(w) for w in ffn_w]
                 + [_const_spec((1, D))],
        out_specs=row(D),
        compiler_params=_params("arbitrary"),
        name="mix_ffn",
    )(xs, mods, gmix, yf, yb, rg, ym, yg, *merge_w, g_ffn, *ffn_w, g_final)


def _rope_tables(L, Cn):
    n_rows = L // GRID_W

    def angles(n, dim):
        half = dim // 2
        inv_freq = ROPE_BASE ** (-jnp.arange(half, dtype=F32) / half)
        ang = jnp.arange(n, dtype=jnp.int32).astype(F32)[:, None] * inv_freq[None, :]
        return jnp.cos(ang), jnp.sin(ang)

    def tables(dim, lane0):
        cr, sr = (jnp.repeat(a, GRID_W, axis=0) for a in angles(n_rows, dim // 2))
        cc, sc = (jnp.tile(a, (n_rows, 1)) for a in angles(GRID_W, dim // 2))
        z = jnp.zeros_like(sr)
        c = jnp.concatenate([cr, cr, cc, cc], axis=1)
        sa = jnp.concatenate([-sr, z, -sc, z], axis=1)
        sb = jnp.concatenate([z, sr, z, sc], axis=1)
        pad = lambda a, fill: jnp.concatenate(
            [jnp.full((L, lane0), fill, F32), a, jnp.full((L, HEAD_LANES - lane0 - dim), fill, F32)], axis=1)
        c, sa, sb = pad(c, 1.0), pad(sa, 0.0), pad(sb, 0.0)
        ctx = lambda fill: jnp.full((Cn, HEAD_LANES), fill, F32)
        return (jnp.concatenate([c, ctx(1.0)]), jnp.concatenate([sa, ctx(0.0)]),
                jnp.concatenate([sb, ctx(0.0)]))

    return tables(RET_DK, 0) + tables(MLA_ROPE, MLA_NOPE)


def _stacked_weights(w_in, w_mla_qb, w_mla_kvb):
    depth = w_in.shape[0]
    split_at = np.cumsum(IN_SIZES)[:-1].tolist()
    rq, rk, rv, rg, cq, ckv, kr, gq, gk, gv, gates = jnp.split(w_in.astype(BF16), split_at, axis=-1)
    kr_pad = jnp.pad(kr, ((0, 0), (0, 0), (MLA_NOPE, HEAD_LANES - MLA_NOPE - MLA_ROPE)))
    qb = w_mla_qb.astype(BF16).reshape(depth, MLA_Q_RANK, MLA_HEADS, MLA_NOPE + MLA_ROPE)
    qb = jnp.pad(qb, ((0, 0), (0, 0), (0, 0), (0, HEAD_LANES - MLA_NOPE - MLA_ROPE)))
    kvb = w_mla_kvb.astype(BF16).reshape(depth, MLA_KV_RANK, MLA_HEADS, MLA_NOPE + MLA_V)
    kb = jnp.pad(kvb[..., :MLA_NOPE], ((0, 0), (0, 0), (0, 0), (0, HEAD_LANES - MLA_NOPE)))
    flat = lambda w: w.reshape(depth, w.shape[1], -1)
    wT = lambda w: jnp.swapaxes(w, 1, 2)
    natural = (rq, rk, rv, rg, ckv, kr_pad, gk, flat(kb))
    transposed = (wT(cq), wT(flat(qb)), wT(flat(kvb[..., MLA_NOPE:])), wT(gq), wT(gv))
    return natural, transposed, gates


def kernel(x, c, ctx, c_ctx, w_mod, b_mod, g_mix, w_in, g_mla_q, g_mla_kv, w_mla_qb, w_mla_kvb,
           g_gqa_q, g_gqa_k, w_ret_o, w_mla_o, w_gqa_o, w_out, g_ffn, w_ffn_in, w_ffn_out, g_final):
    B, L, D = x.shape
    Cn = ctx.shape[1]
    depth = w_mod.shape[0]
    assert B == 1 and D == D_MODEL and L % ROW_TILE == 0 and Cn % ROW_TILE == 0 and L % Cn == 0
    assert L % GRID_W == 0
    n_lat_tiles = L // ROW_TILE
    ffn_hidden = w_ffn_out.shape[1]

    xs = jnp.concatenate([x[0], ctx[0]], axis=0)
    cond = jnp.zeros((MOD_ROWS, D), F32).at[0].set(c[0]).at[1].set(c_ctx)
    mods = _mod_call(cond, w_mod, b_mod)
    mods = mods[:, :2].reshape(depth, 2, N_MOD, D)
    mods = jnp.pad(mods, ((0, 0), (0, 0), (0, MOD_ROWS - N_MOD), (0, 0)))

    tabs = _rope_tables(L, Cn)
    ret_consts = _ret_consts()
    rows3 = lambda g: g.reshape(depth, 1, -1)
    w_nat, w_tr, w_gates = _stacked_weights(w_in, w_mla_qb, w_mla_kvb)
    lanes = lambda g: jnp.broadcast_to(g[:, :, None], g.shape + (ROW_TILE,))
    wts = w_nat + w_tr + (rows3(g_mla_kv), rows3(g_gqa_k), lanes(g_mla_q), lanes(g_gqa_q))
    tabsT = tuple(t.T for t in tabs)
    merge_w = (w_gates,) + tuple(w.astype(BF16) for w in (w_ret_o, w_mla_o, w_gqa_o, w_out))
    w_ffn = w_ffn_in.astype(BF16)
    w_ffn_a, w_ffn_b, w_ffn_o = w_ffn[..., :ffn_hidden], w_ffn[..., ffn_hidden:], w_ffn_out.astype(BF16)
    g_mix3, g_ffn3 = rows3(g_mix), rows3(g_ffn)

    for l in range(depth):
        rq, rk, rv, rg, mqT, mk, mvT, gqT, gk, gvT = _inproj_call(
            xs, mods, g_mix3, tabs, tabsT, wts, n_lat_tiles, l)
        yf, yb = _ret_call(rq, rk, rv, ret_consts, L // RET_CHUNK, Cn // RET_CHUNK)
        ym = _attention(mqT, mk, mvT, 1, L, Cn, "mla")
        yg = _attention(gqT, gk, gvT, GQA_HEADS // GQA_KV_HEADS, L, Cn, "gqa")
        last = l == depth - 1
        xs = _mix_ffn_call(xs, mods, g_mix3, yf, yb, rg, ym, yg, merge_w, g_ffn3,
                           (w_ffn_a, w_ffn_b, w_ffn_o), g_final.reshape(1, -1), n_lat_tiles, l,
                           rows=L if last else L + Cn, final_norm=last)
    return xs[None]
```

```python
import functools
import math

import jax
import jax.numpy as jnp
import numpy as np
from jax import lax
from jax.experimental import pallas as pl
from jax.experimental.pallas import tpu as pltpu

D_MODEL = 1024
GRID_W = 64
RET_CHUNK = 128
ROPE_BASE = 10000.0
NORM_EPS = 1e-6
N_MOD = 6

RET_HEADS = 4
RET_DK = 128
RET_DV = 256
RET_DECAY_START = 5.0
RET_BWD_OFFSET = 0.5

MLA_HEADS = 8
MLA_Q_RANK = 256
MLA_KV_RANK = 256
MLA_NOPE = 64
MLA_ROPE = 32
MLA_V = 128

GQA_HEADS = 8
GQA_KV_HEADS = 2
GQA_HD = 128

RET_W = RET_HEADS * RET_DV
MLA_W = MLA_HEADS * MLA_V
GQA_W = GQA_HEADS * GQA_HD
N_BRANCH = 3

IN_SIZES = (
    RET_HEADS * RET_DK, RET_HEADS * RET_DK, RET_W, RET_W,
    MLA_Q_RANK, MLA_KV_RANK, MLA_ROPE,
    GQA_W, GQA_KV_HEADS * GQA_HD, GQA_KV_HEADS * GQA_HD,
    N_BRANCH * D_MODEL,
)

HEAD_LANES = 128
BF16_SUBLANES = 16
V_ROWS = HEAD_LANES + BF16_SUBLANES
MOD_ROWS = 8
ROW_TILE = 256
FLASH_LAG = 2
FLASH_SUB = 256
V7X_VMEM_LIMIT = 56 * 1024 * 1024
LOG2E = math.log2(math.e)
NEG_BIG = -1e30

F32 = jnp.float32
BF16 = jnp.bfloat16


def _params(*sem, flags=None):
    return pltpu.CompilerParams(dimension_semantics=sem, vmem_limit_bytes=V7X_VMEM_LIMIT, flags=flags)


def _const_spec(shape):
    zeros = (0,) * len(shape)
    return pl.BlockSpec(shape, lambda *_: zeros)


def _layer_spec(w, l, pipeline_mode=None):
    zeros = (0,) * (w.ndim - 1)
    return pl.BlockSpec((None,) + w.shape[1:], lambda *_: (l,) + zeros, pipeline_mode=pipeline_mode)


def _mod_spec(l, n_lat_tiles):
    return pl.BlockSpec((None, 1, MOD_ROWS, D_MODEL), lambda i: (l, i // n_lat_tiles, 0, 0))


def _dot(a, b):
    return jnp.dot(a, b, preferred_element_type=F32)


def _rms(x):
    return x * lax.rsqrt(jnp.mean(x * x, axis=-1, keepdims=True) + NORM_EPS)


def _silu(x):
    return x * jax.nn.sigmoid(x)


def _rope(x, c, sa, sb, block):
    up = pltpu.roll(x, HEAD_LANES - block, 1)
    dn = pltpu.roll(x, block, 1)
    return x * c + up * sa + dn * sb


def _head(x, h):
    return x[:, h * HEAD_LANES:(h + 1) * HEAD_LANES]


def _mod_kernel(cond_ref, w_ref, b_ref, o_ref):
    s = _silu(cond_ref[...]).astype(BF16)
    o_ref[0] = _dot(s, w_ref[0].astype(BF16)) + b_ref[0]


def _mod_call(cond, w_mod, b_mod):
    depth, d, n = w_mod.shape
    tn = n // 4
    return pl.pallas_call(
        _mod_kernel,
        out_shape=jax.ShapeDtypeStruct((depth, MOD_ROWS, n), F32),
        grid=(depth, n // tn),
        in_specs=[
            pl.BlockSpec((MOD_ROWS, d), lambda l, j: (0, 0)),
            pl.BlockSpec((1, d, tn), lambda l, j: (l, 0, j)),
            pl.BlockSpec((1, 1, tn), lambda l, j: (l, 0, j)),
        ],
        out_specs=pl.BlockSpec((1, MOD_ROWS, tn), lambda l, j: (l, 0, j)),
        compiler_params=_params("arbitrary", "arbitrary"),
        name="mod",
    )(cond, w_mod, b_mod.reshape(depth, 1, n))


def _prenorm(x, mod, g, shift_row, scale_row):
    h = _rms(x) * g
    return h * (1.0 + mod[scale_row:scale_row + 1]) + mod[shift_row:shift_row + 1]


def _rms_rows(x):
    return x * lax.rsqrt(jnp.mean(x * x, axis=0, keepdims=True) + NORM_EPS)


def _rope_rows(x, c, sa, sb, block):
    up = jnp.concatenate([x[block:], x[:block]], axis=0)
    dn = jnp.concatenate([x[-block:], x[:-block]], axis=0)
    return x * c + up * sa + dn * sb


def _inproj_kernel(x_ref, mod_ref, gmix_ref, c_ref, sa_ref, sb_ref, cm_ref, sam_ref, sbm_ref,
                   cT_ref, saT_ref, sbT_ref, cmT_ref, samT_ref, sbmT_ref,
                   w_rq, w_rk, w_rv, w_rg, w_ckv, w_kr, w_gk, w_kb,
                   w_cqT, w_qbT, w_vbT, w_gqT, w_gvT, g_mkv, g_gk, g_mqT, g_gqT,
                   rq_o, rk_o, rv_o, rg_o, mqT_o, mk_o, mvT_o, gqT_o, gk_o, gvT_o):
    h = _prenorm(x_ref[...], mod_ref[0], gmix_ref[...], 0, 1)
    hb = h.astype(BF16)
    hTb = h.T.astype(BF16)
    c, sa, sb = c_ref[...], sa_ref[...], sb_ref[...]
    cm, sam, sbm = cm_ref[...], sam_ref[...], sbm_ref[...]
    rope_big = lambda v: _rope(v, c, sa, sb, RET_DK // 4)
    rope_mla = lambda v: _rope(v, cm, sam, sbm, MLA_ROPE // 4)
    rows = lambda v, i: v[i * HEAD_LANES:(i + 1) * HEAD_LANES]
    ones_rows = (lax.broadcasted_iota(jnp.int32, (BF16_SUBLANES, x_ref.shape[0]), 0) == 0).astype(BF16)

    cqnT = (_rms_rows(_dot(w_cqT[...], hTb)) * g_mqT[...]).astype(BF16)
    ckvn = _rms(_dot(hb, w_ckv[...])) * g_mkv[...]

    rq = _dot(hb, w_rq[...])
    rk = _dot(hb, w_rk[...])
    for i in range(RET_HEADS):
        sl = slice(i * HEAD_LANES, (i + 1) * HEAD_LANES)
        rq_o[:, sl] = rope_big(_head(rq, i))
        rk_o[:, sl] = rope_big(_head(rk, i)) * (RET_DK ** -0.5)

    gqa_qscale = GQA_HD ** -0.5 * LOG2E
    gqT = _dot(w_gqT[...], hTb)
    cT, saT, sbT = cT_ref[...], saT_ref[...], sbT_ref[...]
    for i in range(GQA_HEADS):
        qn = _rms_rows(rows(gqT, i)) * g_gqT[...]
        gqT_o[i] = (_rope_rows(qn, cT, saT, sbT, GQA_HD // 4) * gqa_qscale).astype(BF16)
    gk = _dot(hb, w_gk[...])
    gvT = _dot(w_gvT[...], hTb)
    for i in range(GQA_KV_HEADS):
        sl = slice(i * HEAD_LANES, (i + 1) * HEAD_LANES)
        gk_o[:, sl] = rope_big(_rms(_head(gk, i)) * g_gk[...]).astype(BF16)
        gvT_o[i, :HEAD_LANES, :] = rows(gvT, i).astype(BF16)
        gvT_o[i, HEAD_LANES:, :] = ones_rows

    rv_o[...] = _dot(hb, w_rv[...]).astype(BF16)
    rg_o[...] = _dot(hb, w_rg[...])

    mla_qscale = (MLA_NOPE + MLA_ROPE) ** -0.5 * LOG2E
    kr = rope_mla(_dot(hb, w_kr[...]))
    qT = _dot(w_qbT[...], cqnT)
    kn = _dot(ckvn.astype(BF16), w_kb[...])
    vT = _dot(w_vbT[...], ckvn.T.astype(BF16))
    cmT, samT, sbmT = cmT_ref[...], samT_ref[...], sbmT_ref[...]
    for i in range(MLA_HEADS):
        sl = slice(i * HEAD_LANES, (i + 1) * HEAD_LANES)
        mqT_o[i] = (_rope_rows(rows(qT, i), cmT, samT, sbmT, MLA_ROPE // 4) * mla_qscale).astype(BF16)
        mk_o[:, sl] = (_head(kn, i) + kr).astype(BF16)
        mvT_o[i, :HEAD_LANES, :] = rows(vT, i).astype(BF16)
        mvT_o[i, HEAD_LANES:, :] = ones_rows


def _inproj_call(xs, mods, gmix, tabs, tabsT, wts, n_lat_tiles, l):
    T, D = xs.shape
    tm = ROW_TILE
    row = lambda w: pl.BlockSpec((tm, w), lambda i: (i, 0))
    col = pl.BlockSpec((HEAD_LANES, tm), lambda i: (0, i))
    headsT = lambda nh, rows=HEAD_LANES: pl.BlockSpec((nh, rows, tm), lambda i: (0, 0, i))
    in_specs = [row(D), _mod_spec(l, n_lat_tiles), _layer_spec(gmix, l)]
    in_specs += [row(HEAD_LANES)] * 6 + [col] * 6 + [_layer_spec(w, l) for w in wts]
    out_shape = (
        jax.ShapeDtypeStruct((T, RET_HEADS * RET_DK), F32),
        jax.ShapeDtypeStruct((T, RET_HEADS * RET_DK), F32),
        jax.ShapeDtypeStruct((T, RET_W), BF16),
        jax.ShapeDtypeStruct((T, RET_W), F32),
        jax.ShapeDtypeStruct((MLA_HEADS, HEAD_LANES, T), BF16),
        jax.ShapeDtypeStruct((T, MLA_HEADS * HEAD_LANES), BF16),
        jax.ShapeDtypeStruct((MLA_HEADS, V_ROWS, T), BF16),
        jax.ShapeDtypeStruct((GQA_HEADS, HEAD_LANES, T), BF16),
        jax.ShapeDtypeStruct((T, GQA_KV_HEADS * GQA_HD), BF16),
        jax.ShapeDtypeStruct((GQA_KV_HEADS, V_ROWS, T), BF16),
    )
    out_specs = (
        row(RET_HEADS * RET_DK), row(RET_HEADS * RET_DK), row(RET_W), row(RET_W),
        headsT(MLA_HEADS), row(MLA_HEADS * HEAD_LANES), headsT(MLA_HEADS, V_ROWS),
        headsT(GQA_HEADS), row(GQA_KV_HEADS * GQA_HD), headsT(GQA_KV_HEADS, V_ROWS),
    )
    return pl.pallas_call(
        _inproj_kernel,
        out_shape=out_shape,
        grid=(T // tm,),
        in_specs=in_specs,
        out_specs=out_specs,
        compiler_params=_params("arbitrary"),
        name="inproj",
    )(xs, mods, gmix, *tabs, *tabsT, *wts)


def _ret_kernel(qf_ref, kf_ref, vf_ref, qb_ref, kb_ref, vb_ref,
                dec_ref, xi_ref, zeta_ref, cd_ref,
                yf_ref, yb_ref, sf_ref, sb_ref):
    @pl.when(pl.program_id(0) == 0)
    def _():
        sf_ref[...] = jnp.zeros_like(sf_ref)
        sb_ref[...] = jnp.zeros_like(sb_ref)

    dirs = ((qf_ref, kf_ref, vf_ref, yf_ref, sf_ref), (qb_ref, kb_ref, vb_ref, yb_ref, sb_ref))
    pending = []
    for d, (q_ref, k_ref, v_ref, y_ref, s_ref) in enumerate(dirs):
        for h in range(RET_HEADS):
            ksl = slice(h * RET_DK, (h + 1) * RET_DK)
            vsl = slice(h * RET_DV, (h + 1) * RET_DV)
            k = k_ref[:, ksl]
            v = v_ref[:, vsl]
            qb16 = q_ref[:, ksl].astype(BF16)
            scores = lax.dot_general(qb16, k.astype(BF16), (((1,), (1,)), ((), ())),
                                     preferred_element_type=F32) * dec_ref[d, h]
            state = s_ref[h]
            cross = _dot(qb16, state.astype(BF16)) * xi_ref[d, h]
            kz = (k * zeta_ref[d, h]).T.astype(BF16)
            s_ref[h] = cd_ref[d, h] * state + _dot(kz, v)
            pending.append((y_ref, vsl, scores.astype(BF16), v, cross))
    for y_ref, vsl, scores, v, cross in pending:
        y_ref[:, vsl] = _dot(scores, v) + cross


def _ret_call(rq, rk, rv, consts, n_lat, n_ctx):
    T = rq.shape[0]
    C = RET_CHUNK
    n = n_lat + n_ctx

    def fwd(i):
        return jnp.where(i < n_ctx, n_lat + i, i - n_ctx)

    def bwd(i):
        return jnp.where(i < n_ctx, n - 1 - i, n - 1 - i)

    kw, vw = RET_HEADS * RET_DK, RET_W
    spec = lambda w, order: pl.BlockSpec((C, w), lambda i: (order(i), 0))
    dec, xi, zeta, cd = consts
    return pl.pallas_call(
        _ret_kernel,
        out_shape=(jax.ShapeDtypeStruct((T, vw), F32), jax.ShapeDtypeStruct((T, vw), F32)),
        grid=(n,),
        in_specs=[spec(kw, fwd), spec(kw, fwd), spec(vw, fwd),
                  spec(kw, bwd), spec(kw, bwd), spec(vw, bwd),
                  _const_spec(dec.shape), _const_spec(xi.shape),
                  _const_spec(zeta.shape), _const_spec(cd.shape)],
        out_specs=(spec(vw, fwd), spec(vw, bwd)),
        scratch_shapes=[pltpu.VMEM((RET_HEADS, RET_DK, RET_DV), F32),
                        pltpu.VMEM((RET_HEADS, RET_DK, RET_DV), F32)],
        compiler_params=_params("arbitrary"),
        name="retention",
    )(rq, rk, rv, rq, rk, rv, dec, xi, zeta, cd)


def _ret_consts():
    C = RET_CHUNK
    h = jnp.arange(RET_HEADS, dtype=F32)
    pos = jnp.arange(C, dtype=F32)
    diff = pos[:, None] - pos[None, :]

    def one(offset, reverse):
        lg = jnp.log1p(-jnp.exp2(-(RET_DECAY_START + offset) - h))
        dd = -diff if reverse else diff
        keep = dd >= 0
        dec = jnp.where(keep[None], jnp.exp(lg[:, None, None] * jnp.where(keep, dd, 0.0)[None]), 0.0)
        p = (C - 1 - pos) if reverse else pos
        xi = jnp.exp(lg[:, None] * (p + 1)[None, :])
        zeta = jnp.exp(lg[:, None] * (C - 1 - p)[None, :])
        cd = jnp.exp(lg * C)
        return dec, xi[:, :, None], zeta[:, :, None], jnp.broadcast_to(cd[:, None, None], (RET_HEADS, 1, 1))

    f = one(0.0, False)
    b = one(RET_BWD_OFFSET, True)
    return tuple(jnp.stack([a, c]) for a, c in zip(f, b))


def _chunk_colors(nk):
    n_main = (nk // 3) * 3
    if n_main == 0:
        return tuple(range(nk))
    return tuple(c % 3 if c < n_main else 3 + c - n_main for c in range(nk))


def _flash_kernel(qT_ref, k_ref, vT_ref, o_ref, m_ref, acc_ref, *buf_refs, tq, nq, tk, nk, sub):
    lag = FLASH_LAG
    colors = _chunk_colors(nk)
    n_buf = max(colors) + 1
    s_bufs, c_bufs = buf_refs[:n_buf], buf_refs[n_buf:]
    n_sub = tk // sub
    n_loop = max(((nk // 3) * 3 - lag) // 3, 0)

    def key_rows(c, r):
        base = c * tk if isinstance(c, int) else pl.multiple_of(c * tk, tk)
        return pl.ds(base + r * sub, sub)

    def q_cols(i):
        return pl.ds(i * tq if isinstance(i, int) else pl.multiple_of(i * tq, tq), tq)

    def tick(consume, score):
        if consume is not None:
            c, buf = consume
            m_prev = m_ref[...]
            m_new = jnp.maximum(m_prev, c_bufs[buf][...])
            alpha = jnp.exp2(m_prev - m_new)
            m_ref[...] = m_new
        cmax = pv = None
        for r in range(n_sub):
            rows = pl.ds(r * sub, sub)
            if consume is not None:
                p = jnp.exp2(s_bufs[buf][rows, :] - m_new).astype(BF16)
                part = _dot(vT_ref[0, :, key_rows(c, r)], p)
                pv = part if pv is None else pv + part
            if score is not None:
                s = _dot(k_ref[key_rows(score[0], r), :], score[2])
                s_bufs[score[1]][rows, :] = s
                part = jnp.max(s, axis=0, keepdims=True)
                cmax = part if cmax is None else jnp.maximum(cmax, part)
        if score is not None:
            c_bufs[score[1]][...] = cmax
        if consume is not None:
            acc_ref[...] = alpha * acc_ref[...] + pv

    def q_tile(i, carry):
        q = qT_ref[0, :, q_cols(i)]
        nxt = 0 if nq == 1 else jnp.minimum(i + 1, nq - 1)
        q_next = qT_ref[0, :, q_cols(nxt)]
        m_ref[...] = jnp.full_like(m_ref, NEG_BIG)
        acc_ref[...] = jnp.zeros_like(acc_ref)

        def body(t, carry):
            for u in range(3):
                tick((t * 3 + u, u), (t * 3 + u + lag, (u + lag) % 3, q))
            return carry

        if n_loop:
            lax.fori_loop(0, n_loop, body, 0)
        for c in range(3 * n_loop, nk):
            ahead = c + lag
            if ahead < nk:
                score = (ahead, colors[ahead], q)
            elif nq > 1 and ahead - nk < min(lag, nk):
                score = (ahead - nk, colors[ahead - nk], q_next)
            else:
                score = None
            tick((c, colors[c]), score)
        acc = acc_ref[...]
        o_ref[q_cols(i), :] = (acc[:HEAD_LANES] / acc[HEAD_LANES:HEAD_LANES + 1]).T.astype(o_ref.dtype)
        return carry

    q0 = qT_ref[0, :, q_cols(0)]
    for c in range(min(lag, nk)):
        tick(None, (c, colors[c], q0))
    if nq == 1:
        q_tile(0, 0)
    else:
        lax.fori_loop(0, nq, q_tile, 0)


def _flash_kernel_into(qT_ref, k_ref, vT_ref, y_hbm_ref, o_ref, *scratch, **kw):
    del y_hbm_ref
    _flash_kernel(qT_ref, k_ref, vT_ref, o_ref, *scratch, **kw)


def _flash_call(qT, k, vT, y_prev, *, group, q_rows, q_off, kv_rows, kv_off, tq, tk, name):
    hq, T = qT.shape[0], qT.shape[2]
    sub = min(tk, FLASH_SUB)
    assert q_rows % tq == 0 and q_off % q_rows == 0 and kv_rows % tk == 0 and tk % sub == 0
    nq, nk = q_rows // tq, kv_rows // tk
    n_buf = max(_chunk_colors(nk)) + 1
    qb, kb = q_off // q_rows, kv_off // kv_rows
    kw = dict(tq=tq, nq=nq, tk=tk, nk=nk, sub=sub)
    once = pl.Buffered(1)
    in_specs = [
        pl.BlockSpec((1, HEAD_LANES, q_rows), lambda h: (h, 0, qb), pipeline_mode=once),
        pl.BlockSpec((kv_rows, HEAD_LANES), lambda h: (kb, h // group), pipeline_mode=once),
        pl.BlockSpec((1, V_ROWS, kv_rows), lambda h: (h // group, 0, kb), pipeline_mode=once),
    ]
    operands = (qT, k, vT)
    if y_prev is not None:
        in_specs.append(pl.BlockSpec(memory_space=pl.ANY))
        operands += (y_prev,)
    return pl.pallas_call(
        functools.partial(_flash_kernel if y_prev is None else _flash_kernel_into, **kw),
        out_shape=jax.ShapeDtypeStruct((T, hq * HEAD_LANES), BF16),
        grid=(hq,),
        in_specs=in_specs,
        out_specs=pl.BlockSpec((q_rows, HEAD_LANES), lambda h: (qb, h)),
        scratch_shapes=[pltpu.VMEM((1, tq), F32), pltpu.VMEM((V_ROWS, tq), F32)]
                       + [pltpu.VMEM((tk, tq), F32)] * n_buf + [pltpu.VMEM((1, tq), F32)] * n_buf,
        input_output_aliases={} if y_prev is None else {3: 0},
        compiler_params=_params("arbitrary"),
        name=name,
    )(*operands)


def _pick_tile(n, candidates):
    for t in candidates:
        if n % t == 0:
            return t
    raise ValueError(f"no tile for {n}")


def _attention(qT, k, vT, group, L, Cn, name):
    T = L + Cn
    tq = _pick_tile(L, (1024, 512, 256, 128))
    tk = _pick_tile(T, (1280, 256, 128))
    y = _flash_call(qT, k, vT, None, group=group, q_rows=L, q_off=0, kv_rows=T, kv_off=0,
                    tq=tq, tk=tk, name=name)
    return _flash_call(qT, k, vT, y, group=group, q_rows=Cn, q_off=L, kv_rows=Cn, kv_off=L,
                       tq=Cn, tk=Cn, name=name + "_ctx")


def _mix_ffn_kernel(x_ref, mod_ref, gmix_ref, yf_ref, yb_ref, rg_ref, ym_ref, yg_ref,
                    w_gates, w_ro, w_mo, w_go, w_out, gffn_ref, w_a, w_b, w_o, gf_ref, o_ref,
                    *, final_norm):
    x = x_ref[...]
    mod = mod_ref[0]
    hb = _prenorm(x, mod, gmix_ref[...], 0, 1).astype(BF16)
    gs = jax.nn.sigmoid(_dot(hb, w_gates[...]))
    y = yf_ref[...] + yb_ref[...]
    g = _silu(rg_ref[...])
    parts = []
    for h in range(RET_HEADS):
        sl = slice(h * RET_DV, (h + 1) * RET_DV)
        parts.append((g[:, sl] * _rms(y[:, sl])).astype(BF16))
    y_ret = jnp.concatenate(parts, axis=-1)
    D = D_MODEL
    z = (gs[:, :D] * _dot(y_ret, w_ro[...])
         + gs[:, D:2 * D] * _dot(ym_ref[...], w_mo[...])
         + gs[:, 2 * D:] * _dot(yg_ref[...], w_go[...]))
    x = x + mod[2:3] * _dot(z.astype(BF16), w_out[...])

    hb = _prenorm(x, mod, gffn_ref[...], 3, 4).astype(BF16)
    u = (_silu(_dot(hb, w_a[...])) * _dot(hb, w_b[...])).astype(BF16)
    x = x + mod[5:6] * _dot(u, w_o[...])
    o_ref[...] = _rms(x) * gf_ref[...] if final_norm else x


def _mix_ffn_call(xs, mods, gmix, yf, yb, rg, ym, yg, merge_w, g_ffn, ffn_w, g_final, n_lat_tiles, l,
                  rows, final_norm):
    D = xs.shape[1]
    tm = ROW_TILE
    row = lambda w: pl.BlockSpec((tm, w), lambda i: (i, 0))
    once = lambda w: _layer_spec(w, l, pl.Buffered(1))
    return pl.pallas_call(
        functools.partial(_mix_ffn_kernel, final_norm=final_norm),
        out_shape=jax.ShapeDtypeStruct((rows, D), F32),
        grid=(rows // tm,),
        in_specs=[row(D), _mod_spec(l, n_lat_tiles), _layer_spec(gmix, l),
                  row(RET_W), row(RET_W), row(RET_W), row(MLA_W), row(GQA_W)]
                 + [once(w) for w in merge_w] + [_layer_spec(g_ffn, l)] + [once(w) for w in ffn_w]
                 + [_const_spec((1, D))],
        out_specs=row(D),
        compiler_params=_params("arbitrary"),
        name="mix_ffn",
    )(xs, mods, gmix, yf, yb, rg, ym, yg, *merge_w, g_ffn, *ffn_w, g_final)


def _rope_tables(L, Cn):
    n_rows = L // GRID_W

    def angles(n, dim):
        half = dim // 2
        inv_freq = ROPE_BASE ** (-jnp.arange(half, dtype=F32) / half)
        ang = jnp.arange(n, dtype=jnp.int32).astype(F32)[:, None] * inv_freq[None, :]
        return jnp.cos(ang), jnp.sin(ang)

    def tables(dim, lane0):
        cr, sr = (jnp.repeat(a, GRID_W, axis=0) for a in angles(n_rows, dim // 2))
        cc, sc = (jnp.tile(a, (n_rows, 1)) for a in angles(GRID_W, dim // 2))
        z = jnp.zeros_like(sr)
        c = jnp.concatenate([cr, cr, cc, cc], axis=1)
        sa = jnp.concatenate([-sr, z, -sc, z], axis=1)
        sb = jnp.concatenate([z, sr, z, sc], axis=1)
        pad = lambda a, fill: jnp.concatenate(
            [jnp.full((L, lane0), fill, F32), a, jnp.full((L, HEAD_LANES - lane0 - dim), fill, F32)], axis=1)
        c, sa, sb = pad(c, 1.0), pad(sa, 0.0), pad(sb, 0.0)
        ctx = lambda fill: jnp.full((Cn, HEAD_LANES), fill, F32)
        return (jnp.concatenate([c, ctx(1.0)]), jnp.concatenate([sa, ctx(0.0)]),
                jnp.concatenate([sb, ctx(0.0)]))

    return tables(RET_DK, 0) + tables(MLA_ROPE, MLA_NOPE)


def _stacked_weights(w_in, w_mla_qb, w_mla_kvb):
    depth = w_in.shape[0]
    split_at = np.cumsum(IN_SIZES)[:-1].tolist()
    rq, rk, rv, rg, cq, ckv, kr, gq, gk, gv, gates = jnp.split(w_in.astype(BF16), split_at, axis=-1)
    kr_pad = jnp.pad(kr, ((0, 0), (0, 0), (MLA_NOPE, HEAD_LANES - MLA_NOPE - MLA_ROPE)))
    qb = w_mla_qb.astype(BF16).reshape(depth, MLA_Q_RANK, MLA_HEADS, MLA_NOPE + MLA_ROPE)
    qb = jnp.pad(qb, ((0, 0), (0, 0), (0, 0), (0, HEAD_LANES - MLA_NOPE - MLA_ROPE)))
    kvb = w_mla_kvb.astype(BF16).reshape(depth, MLA_KV_RANK, MLA_HEADS, MLA_NOPE + MLA_V)
    kb = jnp.pad(kvb[..., :MLA_NOPE], ((0, 0), (0, 0), (0, 0), (0, HEAD_LANES - MLA_NOPE)))
    flat = lambda w: w.reshape(depth, w.shape[1], -1)
    wT = lambda w: jnp.swapaxes(w, 1, 2)
    natural = (rq, rk, rv, rg, ckv, kr_pad, gk, flat(kb))
    transposed = (wT(cq), wT(flat(qb)), wT(flat(kvb[..., MLA_NOPE:])), wT(gq), wT(gv))
    return natural, transposed, gates


def kernel(x, c, ctx, c_ctx, w_mod, b_mod, g_mix, w_in, g_mla_q, g_mla_kv, w_mla_qb, w_mla_kvb,
           g_gqa_q, g_gqa_k, w_ret_o, w_mla_o, w_gqa_o, w_out, g_ffn, w_ffn_in, w_ffn_out, g_final):
    B, L, D = x.shape
    Cn = ctx.shape[1]
    depth = w_mod.shape[0]
    assert B == 1 and D == D_MODEL and L % ROW_TILE == 0 and Cn % ROW_TILE == 0 and L % Cn == 0
    assert L % GRID_W == 0
    n_lat_tiles = L // ROW_TILE
    ffn_hidden = w_ffn_out.shape[1]

    xs = jnp.concatenate([x[0], ctx[0]], axis=0)
    cond = jnp.zeros((MOD_ROWS, D), F32).at[0].set(c[0]).at[1].set(c_ctx)
    mods = _mod_call(cond, w_mod, b_mod)
    mods = mods[:, :2].reshape(depth, 2, N_MOD, D)
    mods = jnp.pad(mods, ((0, 0), (0, 0), (0, MOD_ROWS - N_MOD), (0, 0)))

    tabs = _rope_tables(L, Cn)
    ret_consts = _ret_consts()
    rows3 = lambda g: g.reshape(depth, 1, -1)
    w_nat, w_tr, w_gates = _stacked_weights(w_in, w_mla_qb, w_mla_kvb)
    lanes = lambda g: jnp.broadcast_to(g[:, :, None], g.shape + (ROW_TILE,))
    wts = w_nat + w_tr + (rows3(g_mla_kv), rows3(g_gqa_k), lanes(g_mla_q), lanes(g_gqa_q))
    tabsT = tuple(t.T for t in tabs)
    merge_w = (w_gates,) + tuple(w.astype(BF16) for w in (w_ret_o, w_mla_o, w_gqa_o, w_out))
    w_ffn = w_ffn_in.astype(BF16)
    w_ffn_a, w_ffn_b, w_ffn_o = w_ffn[..., :ffn_hidden], w_ffn[..., ffn_hidden:], w_ffn_out.astype(BF16)
    g_mix3, g_ffn3 = rows3(g_mix), rows3(g_ffn)

    for l in range(depth):
        rq, rk, rv, rg, mqT, mk, mvT, gqT, gk, gvT = _inproj_call(
            xs, mods, g_mix3, tabs, tabsT, wts, n_lat_tiles, l)
        yf, yb = _ret_call(rq, rk, rv, ret_consts, L // RET_CHUNK, Cn // RET_CHUNK)
        ym = _attention(mqT, mk, mvT, 1, L, Cn, "mla")
        yg = _attention(gqT, gk, gvT, GQA_HEADS // GQA_KV_HEADS, L, Cn, "gqa")
        last = l == depth - 1
        xs = _mix_ffn_call(xs, mods, g_mix3, yf, yb, rg, ym, yg, merge_w, g_ffn3,
                           (w_ffn_a, w_ffn_b, w_ffn_o), g_final.reshape(1, -1), n_lat_tiles, l,
                           rows=L if last else L + Cn, final_norm=last)
    return xs[None]
```

```python
import functools
import math

import jax
import jax.numpy as jnp
import numpy as np
from jax import lax
from jax.experimental import pallas as pl
from jax.experimental.pallas import tpu as pltpu

D_MODEL = 1024
GRID_W = 64
RET_CHUNK = 128
ROPE_BASE = 10000.0
NORM_EPS = 1e-6
N_MOD = 6

RET_HEADS = 4
RET_DK = 128
RET_DV = 256
RET_DECAY_START = 5.0
RET_BWD_OFFSET = 0.5

MLA_HEADS = 8
MLA_Q_RANK = 256
MLA_KV_RANK = 256
MLA_NOPE = 64
MLA_ROPE = 32
MLA_V = 128

GQA_HEADS = 8
GQA_KV_HEADS = 2
GQA_HD = 128

RET_W = RET_HEADS * RET_DV
MLA_W = MLA_HEADS * MLA_V
GQA_W = GQA_HEADS * GQA_HD
N_BRANCH = 3

IN_SIZES = (
    RET_HEADS * RET_DK, RET_HEADS * RET_DK, RET_W, RET_W,
    MLA_Q_RANK, MLA_KV_RANK, MLA_ROPE,
    GQA_W, GQA_KV_HEADS * GQA_HD, GQA_KV_HEADS * GQA_HD,
    N_BRANCH * D_MODEL,
)

HEAD_LANES = 128
BF16_SUBLANES = 16
V_ROWS = HEAD_LANES + BF16_SUBLANES
MOD_ROWS = 8
ROW_TILE = 256
FLASH_LAG = 2
FLASH_SUB = 256
V7X_VMEM_LIMIT = 56 * 1024 * 1024
LOG2E = math.log2(math.e)
NEG_BIG = -1e30

F32 = jnp.float32
BF16 = jnp.bfloat16


def _params(*sem, flags=None):
    return pltpu.CompilerParams(dimension_semantics=sem, vmem_limit_bytes=V7X_VMEM_LIMIT, flags=flags)


def _const_spec(shape):
    zeros = (0,) * len(shape)
    return pl.BlockSpec(shape, lambda *_: zeros)


def _layer_spec(w, l, pipeline_mode=None):
    zeros = (0,) * (w.ndim - 1)
    return pl.BlockSpec((None,) + w.shape[1:], lambda *_: (l,) + zeros, pipeline_mode=pipeline_mode)


def _mod_spec(l, n_lat_tiles):
    return pl.BlockSpec((None, 1, MOD_ROWS, D_MODEL), lambda i: (l, i // n_lat_tiles, 0, 0))


def _dot(a, b):
    return jnp.dot(a, b, preferred_element_type=F32)


def _rms(x):
    return x * lax.rsqrt(jnp.mean(x * x, axis=-1, keepdims=True) + NORM_EPS)


def _silu(x):
    return x * jax.nn.sigmoid(x)


def _rope(x, c, sa, sb, block):
    up = pltpu.roll(x, HEAD_LANES - block, 1)
    dn = pltpu.roll(x, block, 1)
    return x * c + up * sa + dn * sb


def _head(x, h):
    return x[:, h * HEAD_LANES:(h + 1) * HEAD_LANES]


def _mod_kernel(cond_ref, w_ref, b_ref, o_ref):
    s = _silu(cond_ref[...]).astype(BF16)
    o_ref[0] = _dot(s, w_ref[0].astype(BF16)) + b_ref[0]


def _mod_call(cond, w_mod, b_mod):
    depth, d, n = w_mod.shape
    tn = n // 4
    return pl.pallas_call(
        _mod_kernel,
        out_shape=jax.ShapeDtypeStruct((depth, MOD_ROWS, n), F32),
        grid=(depth, n // tn),
        in_specs=[
            pl.BlockSpec((MOD_ROWS, d), lambda l, j: (0, 0)),
            pl.BlockSpec((1, d, tn), lambda l, j: (l, 0, j)),
            pl.BlockSpec((1, 1, tn), lambda l, j: (l, 0, j)),
        ],
        out_specs=pl.BlockSpec((1, MOD_ROWS, tn), lambda l, j: (l, 0, j)),
        compiler_params=_params("arbitrary", "arbitrary"),
        name="mod",
    )(cond, w_mod, b_mod.reshape(depth, 1, n))


def _prenorm(x, mod, g, shift_row, scale_row):
    h = _rms(x) * g
    return h * (1.0 + mod[scale_row:scale_row + 1]) + mod[shift_row:shift_row + 1]


def _rms_rows(x):
    return x * lax.rsqrt(jnp.mean(x * x, axis=0, keepdims=True) + NORM_EPS)


def _rope_rows(x, c, sa, sb, block):
    up = jnp.concatenate([x[block:], x[:block]], axis=0)
    dn = jnp.concatenate([x[-block:], x[:-block]], axis=0)
    return x * c + up * sa + dn * sb


def _inproj_kernel(x_ref, mod_ref, gmix_ref, c_ref, sa_ref, sb_ref, cm_ref, sam_ref, sbm_ref,
                   cT_ref, saT_ref, sbT_ref, cmT_ref, samT_ref, sbmT_ref,
                   w_rq, w_rk, w_rv, w_rg, w_ckv, w_kr, w_gk, w_kb,
                   w_cqT, w_qbT, w_vbT, w_gqT, w_gvT, g_mkv, g_gk, g_mqT, g_gqT,
                   rq_o, rk_o, rv_o, rg_o, mqT_o, mk_o, mvT_o, gqT_o, gk_o, gvT_o):
    h = _prenorm(x_ref[...], mod_ref[0], gmix_ref[...], 0, 1)
    hb = h.astype(BF16)
    hTb = h.T.astype(BF16)
    c, sa, sb = c_ref[...], sa_ref[...], sb_ref[...]
    cm, sam, sbm = cm_ref[...], sam_ref[...], sbm_ref[...]
    rope_big = lambda v: _rope(v, c, sa, sb, RET_DK // 4)
    rope_mla = lambda v: _rope(v, cm, sam, sbm, MLA_ROPE // 4)
    rows = lambda v, i: v[i * HEAD_LANES:(i + 1) * HEAD_LANES]
    ones_rows = (lax.broadcasted_iota(jnp.int32, (BF16_SUBLANES, x_ref.shape[0]), 0) == 0).astype(BF16)

    cqnT = (_rms_rows(_dot(w_cqT[...], hTb)) * g_mqT[...]).astype(BF16)
    ckvn = _rms(_dot(hb, w_ckv[...])) * g_mkv[...]

    rq = _dot(hb, w_rq[...])
    rk = _dot(hb, w_rk[...])
    for i in range(RET_HEADS):
        sl = slice(i * HEAD_LANES, (i + 1) * HEAD_LANES)
        rq_o[:, sl] = rope_big(_head(rq, i))
        rk_o[:, sl] = rope_big(_head(rk, i)) * (RET_DK ** -0.5)

    gqa_qscale = GQA_HD ** -0.5 * LOG2E
    gqT = _dot(w_gqT[...], hTb)
    cT, saT, sbT = cT_ref[...], saT_ref[...], sbT_ref[...]
    for i in range(GQA_HEADS):
        qn = _rms_rows(rows(gqT, i)) * g_gqT[...]
        gqT_o[i] = (_rope_rows(qn, cT, saT, sbT, GQA_HD // 4) * gqa_qscale).astype(BF16)
    gk = _dot(hb, w_gk[...])
    gvT = _dot(w_gvT[...], hTb)
    for i in range(GQA_KV_HEADS):
        sl = slice(i * HEAD_LANES, (i + 1) * HEAD_LANES)
        gk_o[:, sl] = rope_big(_rms(_head(gk, i)) * g_gk[...]).astype(BF16)
        gvT_o[i, :HEAD_LANES, :] = rows(gvT, i).astype(BF16)
        gvT_o[i, HEAD_LANES:, :] = ones_rows

    rv_o[...] = _dot(hb, w_rv[...]).astype(BF16)
    rg_o[...] = _dot(hb, w_rg[...])

    mla_qscale = (MLA_NOPE + MLA_ROPE) ** -0.5 * LOG2E
    kr = rope_mla(_dot(hb, w_kr[...]))
    qT = _dot(w_qbT[...], cqnT)
    kn = _dot(ckvn.astype(BF16), w_kb[...])
    vT = _dot(w_vbT[...], ckvn.T.astype(BF16))
    cmT, samT, sbmT = cmT_ref[...], samT_ref[...], sbmT_ref[...]
    for i in range(MLA_HEADS):
        sl = slice(i * HEAD_LANES, (i + 1) * HEAD_LANES)
        mqT_o[i] = (_rope_rows(rows(qT, i), cmT, samT, sbmT, MLA_ROPE // 4) * mla_qscale).astype(BF16)
        mk_o[:, sl] = (_head(kn, i) + kr).astype(BF16)
        mvT_o[i, :HEAD_LANES, :] = rows(vT, i).astype(BF16)
        mvT_o[i, HEAD_LANES:, :] = ones_rows


def _inproj_call(xs, mods, gmix, tabs, tabsT, wts, n_lat_tiles, l):
    T, D = xs.shape
    tm = ROW_TILE
    row = lambda w: pl.BlockSpec((tm, w), lambda i: (i, 0))
    col = pl.BlockSpec((HEAD_LANES, tm), lambda i: (0, i))
    headsT = lambda nh, rows=HEAD_LANES: pl.BlockSpec((nh, rows, tm), lambda i: (0, 0, i))
    in_specs = [row(D), _mod_spec(l, n_lat_tiles), _layer_spec(gmix, l)]
    in_specs += [row(HEAD_LANES)] * 6 + [col] * 6 + [_layer_spec(w, l) for w in wts]
    out_shape = (
        jax.ShapeDtypeStruct((T, RET_HEADS * RET_DK), F32),
        jax.ShapeDtypeStruct((T, RET_HEADS * RET_DK), F32),
        jax.ShapeDtypeStruct((T, RET_W), BF16),
        jax.ShapeDtypeStruct((T, RET_W), F32),
        jax.ShapeDtypeStruct((MLA_HEADS, HEAD_LANES, T), BF16),
        jax.ShapeDtypeStruct((T, MLA_HEADS * HEAD_LANES), BF16),
        jax.ShapeDtypeStruct((MLA_HEADS, V_ROWS, T), BF16),
        jax.ShapeDtypeStruct((GQA_HEADS, HEAD_LANES, T), BF16),
        jax.ShapeDtypeStruct((T, GQA_KV_HEADS * GQA_HD), BF16),
        jax.ShapeDtypeStruct((GQA_KV_HEADS, V_ROWS, T), BF16),
    )
    out_specs = (
        row(RET_HEADS * RET_DK), row(RET_HEADS * RET_DK), row(RET_W), row(RET_W),
        headsT(MLA_HEADS), row(MLA_HEADS * HEAD_LANES), headsT(MLA_HEADS, V_ROWS),
        headsT(GQA_HEADS), row(GQA_KV_HEADS * GQA_HD), headsT(GQA_KV_HEADS, V_ROWS),
    )
    return pl.pallas_call(
        _inproj_kernel,
        out_shape=out_shape,
        grid=(T // tm,),
        in_specs=in_specs,
        out_specs=out_specs,
        compiler_params=_params("arbitrary"),
        name="inproj",
    )(xs, mods, gmix, *tabs, *tabsT, *wts)


RET_CHUNKS_PER_STEP = 2


def _ret_kernel(qf_ref, kf_ref, vf_ref, qb_ref, kb_ref, vb_ref,
                dec_ref, xi_ref, zeta_ref, cd_ref,
                yf_ref, yb_ref, sf_ref, sb_ref):
    @pl.when(pl.program_id(0) == 0)
    def _():
        sf_ref[...] = jnp.zeros_like(sf_ref)
        sb_ref[...] = jnp.zeros_like(sb_ref)

    C = RET_CHUNK
    dirs = ((qf_ref, kf_ref, vf_ref, yf_ref, sf_ref, range(RET_CHUNKS_PER_STEP)),
            (qb_ref, kb_ref, vb_ref, yb_ref, sb_ref, range(RET_CHUNKS_PER_STEP - 1, -1, -1)))
    states = [[s_ref[h] for h in range(RET_HEADS)] for *_, s_ref, _ in dirs]
    pending = []
    for pos in range(RET_CHUNKS_PER_STEP):
        for d, (q_ref, k_ref, v_ref, y_ref, s_ref, order) in enumerate(dirs):
            rows = pl.ds(order[pos] * C, C)
            for h in range(RET_HEADS):
                ksl = slice(h * RET_DK, (h + 1) * RET_DK)
                vsl = slice(h * RET_DV, (h + 1) * RET_DV)
                k = k_ref[rows, ksl]
                v = v_ref[rows, vsl]
                qb16 = q_ref[rows, ksl].astype(BF16)
                scores = lax.dot_general(qb16, k.astype(BF16), (((1,), (1,)), ((), ())),
                                         preferred_element_type=F32) * dec_ref[d, h]
                state = states[d][h]
                cross = _dot(qb16, state.astype(BF16)) * xi_ref[d, h]
                kz = (k * zeta_ref[d, h]).T.astype(BF16)
                states[d][h] = cd_ref[d, h] * state + _dot(kz, v)
                pending.append((y_ref, rows, vsl, scores.astype(BF16), v, cross))
    for d, (*_, s_ref, _) in enumerate(dirs):
        for h in range(RET_HEADS):
            s_ref[h] = states[d][h]
    for y_ref, rows, vsl, scores, v, cross in pending:
        y_ref[rows, vsl] = _dot(scores, v) + cross


def _ret_call(rq, rk, rv, consts, n_lat, n_ctx):
    T = rq.shape[0]
    rows = RET_CHUNK * RET_CHUNKS_PER_STEP
    assert n_lat % RET_CHUNKS_PER_STEP == 0 and n_ctx % RET_CHUNKS_PER_STEP == 0
    g_lat, g_ctx = n_lat // RET_CHUNKS_PER_STEP, n_ctx // RET_CHUNKS_PER_STEP
    n = g_lat + g_ctx

    def fwd(i):
        return jnp.where(i < g_ctx, g_lat + i, i - g_ctx)

    def bwd(i):
        return n - 1 - i

    kw, vw = RET_HEADS * RET_DK, RET_W
    spec = lambda w, order: pl.BlockSpec((rows, w), lambda i: (order(i), 0))
    dec, xi, zeta, cd = consts
    return pl.pallas_call(
        _ret_kernel,
        out_shape=(jax.ShapeDtypeStruct((T, vw), F32), jax.ShapeDtypeStruct((T, vw), F32)),
        grid=(n,),
        in_specs=[spec(kw, fwd), spec(kw, fwd), spec(vw, fwd),
                  spec(kw, bwd), spec(kw, bwd), spec(vw, bwd),
                  _const_spec(dec.shape), _const_spec(xi.shape),
                  _const_spec(zeta.shape), _const_spec(cd.shape)],
        out_specs=(spec(vw, fwd), spec(vw, bwd)),
        scratch_shapes=[pltpu.VMEM((RET_HEADS, RET_DK, RET_DV), F32),
                        pltpu.VMEM((RET_HEADS, RET_DK, RET_DV), F32)],
        compiler_params=_params("arbitrary"),
        name="retention",
    )(rq, rk, rv, rq, rk, rv, dec, xi, zeta, cd)


def _ret_consts():
    C = RET_CHUNK
    h = jnp.arange(RET_HEADS, dtype=F32)
    pos = jnp.arange(C, dtype=F32)
    diff = pos[:, None] - pos[None, :]

    def one(offset, reverse):
        lg = jnp.log1p(-jnp.exp2(-(RET_DECAY_START + offset) - h))
        dd = -diff if reverse else diff
        keep = dd >= 0
        dec = jnp.where(keep[None], jnp.exp(lg[:, None, None] * jnp.where(keep, dd, 0.0)[None]), 0.0)
        p = (C - 1 - pos) if reverse else pos
        xi = jnp.exp(lg[:, None] * (p + 1)[None, :])
        zeta = jnp.exp(lg[:, None] * (C - 1 - p)[None, :])
        cd = jnp.exp(lg * C)
        return dec, xi[:, :, None], zeta[:, :, None], jnp.broadcast_to(cd[:, None, None], (RET_HEADS, 1, 1))

    f = one(0.0, False)
    b = one(RET_BWD_OFFSET, True)
    return tuple(jnp.stack([a, c]) for a, c in zip(f, b))


def _chunk_colors(nk):
    n_main = (nk // 3) * 3
    if n_main == 0:
        return tuple(range(nk))
    return tuple(c % 3 if c < n_main else 3 + c - n_main for c in range(nk))


def _flash_kernel(qT_ref, k_ref, vT_ref, o_ref, m_ref, acc_ref, *buf_refs, tq, nq, tk, nk, sub):
    lag = FLASH_LAG
    colors = _chunk_colors(nk)
    n_buf = max(colors) + 1
    s_bufs, c_bufs = buf_refs[:n_buf], buf_refs[n_buf:]
    n_sub = tk // sub
    n_loop = max(((nk // 3) * 3 - lag) // 3, 0)

    def key_rows(c, r):
        base = c * tk if isinstance(c, int) else pl.multiple_of(c * tk, tk)
        return pl.ds(base + r * sub, sub)

    def q_cols(i):
        return pl.ds(i * tq if isinstance(i, int) else pl.multiple_of(i * tq, tq), tq)

    def tick(consume, score):
        if consume is not None:
            c, buf = consume
            m_prev = m_ref[...]
            m_new = jnp.maximum(m_prev, c_bufs[buf][...])
            alpha = jnp.exp2(m_prev - m_new)
            m_ref[...] = m_new
        cmax = pv = None
        for r in range(n_sub):
            rows = pl.ds(r * sub, sub)
            if consume is not None:
                p = jnp.exp2(s_bufs[buf][rows, :] - m_new).astype(BF16)
                part = _dot(vT_ref[0, :, key_rows(c, r)], p)
                pv = part if pv is None else pv + part
            if score is not None:
                s = _dot(k_ref[key_rows(score[0], r), :], score[2])
                s_bufs[score[1]][rows, :] = s
                part = jnp.max(s, axis=0, keepdims=True)
                cmax = part if cmax is None else jnp.maximum(cmax, part)
        if score is not None:
            c_bufs[score[1]][...] = cmax
        if consume is not None:
            acc_ref[...] = alpha * acc_ref[...] + pv

    def q_tile(i, carry):
        q = qT_ref[0, :, q_cols(i)]
        nxt = 0 if nq == 1 else jnp.minimum(i + 1, nq - 1)
        q_next = qT_ref[0, :, q_cols(nxt)]
        m_ref[...] = jnp.full_like(m_ref, NEG_BIG)
        acc_ref[...] = jnp.zeros_like(acc_ref)

        def body(t, carry):
            for u in range(3):
                tick((t * 3 + u, u), (t * 3 + u + lag, (u + lag) % 3, q))
            return carry

        if n_loop:
            lax.fori_loop(0, n_loop, body, 0)
        for c in range(3 * n_loop, nk):
            ahead = c + lag
            if ahead < nk:
                score = (ahead, colors[ahead], q)
            elif nq > 1 and ahead - nk < min(lag, nk):
                score = (ahead - nk, colors[ahead - nk], q_next)
            else:
                score = None
            tick((c, colors[c]), score)
        acc = acc_ref[...]
        o_ref[q_cols(i), :] = (acc[:HEAD_LANES] / acc[HEAD_LANES:HEAD_LANES + 1]).T.astype(o_ref.dtype)
        return carry

    q0 = qT_ref[0, :, q_cols(0)]
    for c in range(min(lag, nk)):
        tick(None, (c, colors[c], q0))
    if nq == 1:
        q_tile(0, 0)
    else:
        lax.fori_loop(0, nq, q_tile, 0)


def _flash_kernel_into(qT_ref, k_ref, vT_ref, y_hbm_ref, o_ref, *scratch, **kw):
    del y_hbm_ref
    _flash_kernel(qT_ref, k_ref, vT_ref, o_ref, *scratch, **kw)


def _flash_call(qT, k, vT, y_prev, *, group, q_rows, q_off, kv_rows, kv_off, tq, tk, name):
    hq, T = qT.shape[0], qT.shape[2]
    sub = min(tk, FLASH_SUB)
    assert q_rows % tq == 0 and q_off % q_rows == 0 and kv_rows % tk == 0 and tk % sub == 0
    nq, nk = q_rows // tq, kv_rows // tk
    n_buf = max(_chunk_colors(nk)) + 1
    qb, kb = q_off // q_rows, kv_off // kv_rows
    kw = dict(tq=tq, nq=nq, tk=tk, nk=nk, sub=sub)
    in_specs = [
        pl.BlockSpec((1, HEAD_LANES, q_rows), lambda h: (h, 0, qb), pipeline_mode=pl.Buffered(1)),
        pl.BlockSpec((kv_rows, HEAD_LANES), lambda h: (kb, h // group)),
        pl.BlockSpec((1, V_ROWS, kv_rows), lambda h: (h // group, 0, kb)),
    ]
    operands = (qT, k, vT)
    if y_prev is not None:
        in_specs.append(pl.BlockSpec(memory_space=pl.ANY))
        operands += (y_prev,)
    return pl.pallas_call(
        functools.partial(_flash_kernel if y_prev is None else _flash_kernel_into, **kw),
        out_shape=jax.ShapeDtypeStruct((T, hq * HEAD_LANES), BF16),
        grid=(hq,),
        in_specs=in_specs,
        out_specs=pl.BlockSpec((q_rows, HEAD_LANES), lambda h: (qb, h)),
        scratch_shapes=[pltpu.VMEM((1, tq), F32), pltpu.VMEM((V_ROWS, tq), F32)]
                       + [pltpu.VMEM((tk, tq), F32)] * n_buf + [pltpu.VMEM((1, tq), F32)] * n_buf,
        input_output_aliases={} if y_prev is None else {3: 0},
        compiler_params=_params("arbitrary"),
        name=name,
    )(*operands)


def _pick_tile(n, candidates):
    for t in candidates:
        if n % t == 0:
            return t
    raise ValueError(f"no tile for {n}")


def _attention(qT, k, vT, group, L, Cn, name):
    T = L + Cn
    tq = _pick_tile(L, (1024, 512, 256, 128))
    tk = _pick_tile(T, (1280, 256, 128))
    y = _flash_call(qT, k, vT, None, group=group, q_rows=L, q_off=0, kv_rows=T, kv_off=0,
                    tq=tq, tk=tk, name=name)
    return _flash_call(qT, k, vT, y, group=group, q_rows=Cn, q_off=L, kv_rows=Cn, kv_off=L,
                       tq=Cn, tk=Cn, name=name + "_ctx")


def _mix_ffn_kernel(x_ref, mod_ref, gmix_ref, yf_ref, yb_ref, rg_ref, ym_ref, yg_ref,
                    w_gates, w_ro, w_mo, w_go, w_out, gffn_ref, w_a, w_b, w_o, gf_ref, o_ref,
                    *, final_norm):
    x = x_ref[...]
    mod = mod_ref[0]
    hb = _prenorm(x, mod, gmix_ref[...], 0, 1).astype(BF16)
    gs = jax.nn.sigmoid(_dot(hb, w_gates[...]))
    y = yf_ref[...] + yb_ref[...]
    g = _silu(rg_ref[...])
    parts = []
    for h in range(RET_HEADS):
        sl = slice(h * RET_DV, (h + 1) * RET_DV)
        parts.append((g[:, sl] * _rms(y[:, sl])).astype(BF16))
    y_ret = jnp.concatenate(parts, axis=-1)
    D = D_MODEL
    z = (gs[:, :D] * _dot(y_ret, w_ro[...])
         + gs[:, D:2 * D] * _dot(ym_ref[...], w_mo[...])
         + gs[:, 2 * D:] * _dot(yg_ref[...], w_go[...]))
    x = x + mod[2:3] * _dot(z.astype(BF16), w_out[...])

    hb = _prenorm(x, mod, gffn_ref[...], 3, 4).astype(BF16)
    u = (_silu(_dot(hb, w_a[...])) * _dot(hb, w_b[...])).astype(BF16)
    x = x + mod[5:6] * _dot(u, w_o[...])
    o_ref[...] = _rms(x) * gf_ref[...] if final_norm else x


def _mix_ffn_call(xs, mods, gmix, yf, yb, rg, ym, yg, merge_w, g_ffn, ffn_w, g_final, n_lat_tiles, l,
                  rows, final_norm):
    D = xs.shape[1]
    tm = ROW_TILE
    row = lambda w: pl.BlockSpec((tm, w), lambda i: (i, 0))
    once = lambda w: _layer_spec(w, l, pl.Buffered(1))
    return pl.pallas_call(
        functools.partial(_mix_ffn_kernel, final_norm=final_norm),
        out_shape=jax.ShapeDtypeStruct((rows, D), F32),
        grid=(rows // tm,),
        in_specs=[row(D), _mod_spec(l, n_lat_tiles), _layer_spec(gmix, l),
                  row(RET_W), row(RET_W), row(RET_W), row(MLA_W), row(GQA_W)]
                 + [once(w) for w in merge_w] + [_layer_spec(g_ffn, l)] + [once(w) for w in ffn_w]
                 + [_const_spec((1, D))],
        out_specs=row(D),
        compiler_params=_params("arbitrary"),
        name="mix_ffn",
    )(xs, mods, gmix, yf, yb, rg, ym, yg, *merge_w, g_ffn, *ffn_w, g_final)


def _rope_tables(L, Cn):
    n_rows = L // GRID_W

    def angles(n, dim):
        half = dim // 2
        inv_freq = ROPE_BASE ** (-jnp.arange(half, dtype=F32) / half)
        ang = jnp.arange(n, dtype=jnp.int32).astype(F32)[:, None] * inv_freq[None, :]
        return jnp.cos(ang), jnp.sin(ang)

    def tables(dim, lane0):
        cr, sr = (jnp.repeat(a, GRID_W, axis=0) for a in angles(n_rows, dim // 2))
        cc, sc = (jnp.tile(a, (n_rows, 1)) for a in angles(GRID_W, dim // 2))
        z = jnp.zeros_like(sr)
        c = jnp.concatenate([cr, cr, cc, cc], axis=1)
        sa = jnp.concatenate([-sr, z, -sc, z], axis=1)
        sb = jnp.concatenate([z, sr, z, sc], axis=1)
        pad = lambda a, fill: jnp.concatenate(
            [jnp.full((L, lane0), fill, F32), a, jnp.full((L, HEAD_LANES - lane0 - dim), fill, F32)], axis=1)
        c, sa, sb = pad(c, 1.0), pad(sa, 0.0), pad(sb, 0.0)
        ctx = lambda fill: jnp.full((Cn, HEAD_LANES), fill, F32)
        return (jnp.concatenate([c, ctx(1.0)]), jnp.concatenate([sa, ctx(0.0)]),
                jnp.concatenate([sb, ctx(0.0)]))

    return tables(RET_DK, 0) + tables(MLA_ROPE, MLA_NOPE)


def _stacked_weights(w_in, w_mla_qb, w_mla_kvb):
    depth = w_in.shape[0]
    split_at = np.cumsum(IN_SIZES)[:-1].tolist()
    rq, rk, rv, rg, cq, ckv, kr, gq, gk, gv, gates = jnp.split(w_in.astype(BF16), split_at, axis=-1)
    kr_pad = jnp.pad(kr, ((0, 0), (0, 0), (MLA_NOPE, HEAD_LANES - MLA_NOPE - MLA_ROPE)))
    qb = w_mla_qb.astype(BF16).reshape(depth, MLA_Q_RANK, MLA_HEADS, MLA_NOPE + MLA_ROPE)
    qb = jnp.pad(qb, ((0, 0), (0, 0), (0, 0), (0, HEAD_LANES - MLA_NOPE - MLA_ROPE)))
    kvb = w_mla_kvb.astype(BF16).reshape(depth, MLA_KV_RANK, MLA_HEADS, MLA_NOPE + MLA_V)
    kb = jnp.pad(kvb[..., :MLA_NOPE], ((0, 0), (0, 0), (0, 0), (0, HEAD_LANES - MLA_NOPE)))
    flat = lambda w: w.reshape(depth, w.shape[1], -1)
    wT = lambda w: jnp.swapaxes(w, 1, 2)
    natural = (rq, rk, rv, rg, ckv, kr_pad, gk, flat(kb))
    transposed = (wT(cq), wT(flat(qb)), wT(flat(kvb[..., MLA_NOPE:])), wT(gq), wT(gv))
    return natural, transposed, gates


def kernel(x, c, ctx, c_ctx, w_mod, b_mod, g_mix, w_in, g_mla_q, g_mla_kv, w_mla_qb, w_mla_kvb,
           g_gqa_q, g_gqa_k, w_ret_o, w_mla_o, w_gqa_o, w_out, g_ffn, w_ffn_in, w_ffn_out, g_final):
    B, L, D = x.shape
    Cn = ctx.shape[1]
    depth = w_mod.shape[0]
    assert B == 1 and D == D_MODEL and L % ROW_TILE == 0 and Cn % ROW_TILE == 0 and L % Cn == 0
    assert L % GRID_W == 0
    n_lat_tiles = L // ROW_TILE
    ffn_hidden = w_ffn_out.shape[1]

    xs = jnp.concatenate([x[0], ctx[0]], axis=0)
    cond = jnp.zeros((MOD_ROWS, D), F32).at[0].set(c[0]).at[1].set(c_ctx)
    mods = _mod_call(cond, w_mod, b_mod)
    mods = mods[:, :2].reshape(depth, 2, N_MOD, D)
    mods = jnp.pad(mods, ((0, 0), (0, 0), (0, MOD_ROWS - N_MOD), (0, 0)))

    tabs = _rope_tables(L, Cn)
    ret_consts = _ret_consts()
    rows3 = lambda g: g.reshape(depth, 1, -1)
    w_nat, w_tr, w_gates = _stacked_weights(w_in, w_mla_qb, w_mla_kvb)
    lanes = lambda g: jnp.broadcast_to(g[:, :, None], g.shape + (ROW_TILE,))
    wts = w_nat + w_tr + (rows3(g_mla_kv), rows3(g_gqa_k), lanes(g_mla_q), lanes(g_gqa_q))
    tabsT = tuple(t.T for t in tabs)
    merge_w = (w_gates,) + tuple(w.astype(BF16) for w in (w_ret_o, w_mla_o, w_gqa_o, w_out))
    w_ffn = w_ffn_in.astype(BF16)
    w_ffn_a, w_ffn_b, w_ffn_o = w_ffn[..., :ffn_hidden], w_ffn[..., ffn_hidden:], w_ffn_out.astype(BF16)
    g_mix3, g_ffn3 = rows3(g_mix), rows3(g_ffn)

    for l in range(depth):
        rq, rk, rv, rg, mqT, mk, mvT, gqT, gk, gvT = _inproj_call(
            xs, mods, g_mix3, tabs, tabsT, wts, n_lat_tiles, l)
        yf, yb = _ret_call(rq, rk, rv, ret_consts, L // RET_CHUNK, Cn // RET_CHUNK)
        ym = _attention(mqT, mk, mvT, 1, L, Cn, "mla")
        yg = _attention(gqT, gk, gvT, GQA_HEADS // GQA_KV_HEADS, L, Cn, "gqa")
        last = l == depth - 1
        xs = _mix_ffn_call(xs, mods, g_mix3, yf, yb, rg, ym, yg, merge_w, g_ffn3,
                           (w_ffn_a, w_ffn_b, w_ffn_o), g_final.reshape(1, -1), n_lat_tiles, l,
                           rows=L if last else L + Cn, final_norm=last)
    return xs[None]
```

```python
import functools
import math

import jax
import jax.numpy as jnp
import numpy as np
from jax import lax
from jax.experimental import pallas as pl
from jax.experimental.pallas import tpu as pltpu

D_MODEL = 1024
GRID_W = 64
RET_CHUNK = 128
ROPE_BASE = 10000.0
NORM_EPS = 1e-6
N_MOD = 6

RET_HEADS = 4
RET_DK = 128
RET_DV = 256
RET_DECAY_START = 5.0
RET_BWD_OFFSET = 0.5

MLA_HEADS = 8
MLA_Q_RANK = 256
MLA_KV_RANK = 256
MLA_NOPE = 64
MLA_ROPE = 32
MLA_V = 128

GQA_HEADS = 8
GQA_KV_HEADS = 2
GQA_HD = 128

RET_W = RET_HEADS * RET_DV
MLA_W = MLA_HEADS * MLA_V
GQA_W = GQA_HEADS * GQA_HD
N_BRANCH = 3

IN_SIZES = (
    RET_HEADS * RET_DK, RET_HEADS * RET_DK, RET_W, RET_W,
    MLA_Q_RANK, MLA_KV_RANK, MLA_ROPE,
    GQA_W, GQA_KV_HEADS * GQA_HD, GQA_KV_HEADS * GQA_HD,
    N_BRANCH * D_MODEL,
)

HEAD_LANES = 128
BF16_SUBLANES = 16
V_ROWS = HEAD_LANES + BF16_SUBLANES
MOD_ROWS = 8
ROW_TILE = 256
FLASH_LAG = 2
FLASH_SUB = 256
V7X_VMEM_LIMIT = 56 * 1024 * 1024
LOG2E = math.log2(math.e)
NEG_BIG = -1e30

F32 = jnp.float32
BF16 = jnp.bfloat16


def _params(*sem):
    return pltpu.CompilerParams(dimension_semantics=sem, vmem_limit_bytes=V7X_VMEM_LIMIT)


def _const_spec(shape):
    zeros = (0,) * len(shape)
    return pl.BlockSpec(shape, lambda *_: zeros)


def _layer_spec(w, l, pipeline_mode=None):
    zeros = (0,) * (w.ndim - 1)
    return pl.BlockSpec((None,) + w.shape[1:], lambda *_: (l,) + zeros, pipeline_mode=pipeline_mode)


def _mod_spec(l, n_lat_tiles):
    return pl.BlockSpec((None, 1, MOD_ROWS, D_MODEL), lambda i: (l, i // n_lat_tiles, 0, 0))


def _dot(a, b):
    return jnp.dot(a, b, preferred_element_type=F32)


def _rms(x):
    return x * lax.rsqrt(jnp.mean(x * x, axis=-1, keepdims=True) + NORM_EPS)


def _silu(x):
    return x * jax.nn.sigmoid(x)


def _rope(x, c, sa, sb, block):
    up = pltpu.roll(x, HEAD_LANES - block, 1)
    dn = pltpu.roll(x, block, 1)
    return x * c + up * sa + dn * sb


def _head(x, h):
    return x[:, h * HEAD_LANES:(h + 1) * HEAD_LANES]


def _mod_kernel(cond_ref, w_ref, b_ref, o_ref):
    s = _silu(cond_ref[...]).astype(BF16)
    o_ref[0] = _dot(s, w_ref[0].astype(BF16)) + b_ref[0]


def _mod_call(cond, w_mod, b_mod):
    depth, d, n = w_mod.shape
    tn = n // 4
    return pl.pallas_call(
        _mod_kernel,
        out_shape=jax.ShapeDtypeStruct((depth, MOD_ROWS, n), F32),
        grid=(depth, n // tn),
        in_specs=[
            pl.BlockSpec((MOD_ROWS, d), lambda l, j: (0, 0)),
            pl.BlockSpec((1, d, tn), lambda l, j: (l, 0, j)),
            pl.BlockSpec((1, 1, tn), lambda l, j: (l, 0, j)),
        ],
        out_specs=pl.BlockSpec((1, MOD_ROWS, tn), lambda l, j: (l, 0, j)),
        compiler_params=_params("arbitrary", "arbitrary"),
        name="mod",
    )(cond, w_mod, b_mod.reshape(depth, 1, n))


def _prenorm(x, mod, g, shift_row, scale_row):
    h = _rms(x) * g
    return h * (1.0 + mod[scale_row:scale_row + 1]) + mod[shift_row:shift_row + 1]


def _rms_rows(x):
    return x * lax.rsqrt(jnp.mean(x * x, axis=0, keepdims=True) + NORM_EPS)


def _rope_rows(x, c, sa, sb, block):
    up = jnp.concatenate([x[block:], x[:block]], axis=0)
    dn = jnp.concatenate([x[-block:], x[:-block]], axis=0)
    return x * c + up * sa + dn * sb


def _inproj_kernel(x_ref, mod_ref, gmix_ref, c_ref, sa_ref, sb_ref, cm_ref, sam_ref, sbm_ref,
                   cT_ref, saT_ref, sbT_ref, cmT_ref, samT_ref, sbmT_ref,
                   w_rq, w_rk, w_rv, w_rg, w_ckv, w_kr, w_gk, w_kb,
                   w_cqT, w_qbT, w_vbT, w_gqT, w_gvT, g_mkv, g_gk, g_mqT, g_gqT,
                   rq_o, rk_o, rv_o, rg_o, mqT_o, mk_o, mvT_o, gqT_o, gk_o, gvT_o):
    h = _prenorm(x_ref[...], mod_ref[0], gmix_ref[...], 0, 1)
    hb = h.astype(BF16)
    hTb = h.T.astype(BF16)
    c, sa, sb = c_ref[...], sa_ref[...], sb_ref[...]
    cm, sam, sbm = cm_ref[...], sam_ref[...], sbm_ref[...]
    rope_big = lambda v: _rope(v, c, sa, sb, RET_DK // 4)
    rope_mla = lambda v: _rope(v, cm, sam, sbm, MLA_ROPE // 4)
    rows = lambda v, i: v[i * HEAD_LANES:(i + 1) * HEAD_LANES]
    ones_rows = (lax.broadcasted_iota(jnp.int32, (BF16_SUBLANES, x_ref.shape[0]), 0) == 0).astype(BF16)

    cqnT = (_rms_rows(_dot(w_cqT[...], hTb)) * g_mqT[...]).astype(BF16)
    ckvn = _rms(_dot(hb, w_ckv[...])) * g_mkv[...]

    rq = _dot(hb, w_rq[...])
    rk = _dot(hb, w_rk[...])
    for i in range(RET_HEADS):
        sl = slice(i * HEAD_LANES, (i + 1) * HEAD_LANES)
        rq_o[:, sl] = rope_big(_head(rq, i))
        rk_o[:, sl] = rope_big(_head(rk, i)) * (RET_DK ** -0.5)

    gqa_qscale = GQA_HD ** -0.5 * LOG2E
    gqT = _dot(w_gqT[...], hTb)
    cT, saT, sbT = cT_ref[...], saT_ref[...], sbT_ref[...]
    for i in range(GQA_HEADS):
        qn = _rms_rows(rows(gqT, i)) * g_gqT[...]
        gqT_o[i] = (_rope_rows(qn, cT, saT, sbT, GQA_HD // 4) * gqa_qscale).astype(BF16)
    gk = _dot(hb, w_gk[...])
    gvT = _dot(w_gvT[...], hTb)
    for i in range(GQA_KV_HEADS):
        sl = slice(i * HEAD_LANES, (i + 1) * HEAD_LANES)
        gk_o[:, sl] = rope_big(_rms(_head(gk, i)) * g_gk[...]).astype(BF16)
        gvT_o[i, :HEAD_LANES, :] = rows(gvT, i).astype(BF16)
        gvT_o[i, HEAD_LANES:, :] = ones_rows

    rv_o[...] = _dot(hb, w_rv[...]).astype(BF16)
    rg_o[...] = _dot(hb, w_rg[...])

    mla_qscale = (MLA_NOPE + MLA_ROPE) ** -0.5 * LOG2E
    kr = rope_mla(_dot(hb, w_kr[...]))
    qT = _dot(w_qbT[...], cqnT)
    kn = _dot(ckvn.astype(BF16), w_kb[...])
    vT = _dot(w_vbT[...], ckvn.T.astype(BF16))
    cmT, samT, sbmT = cmT_ref[...], samT_ref[...], sbmT_ref[...]
    for i in range(MLA_HEADS):
        sl = slice(i * HEAD_LANES, (i + 1) * HEAD_LANES)
        mqT_o[i] = (_rope_rows(rows(qT, i), cmT, samT, sbmT, MLA_ROPE // 4) * mla_qscale).astype(BF16)
        mk_o[:, sl] = (_head(kn, i) + kr).astype(BF16)
        mvT_o[i, :HEAD_LANES, :] = rows(vT, i).astype(BF16)
        mvT_o[i, HEAD_LANES:, :] = ones_rows


def _inproj_call(xs, mods, gmix, tabs, tabsT, wts, n_lat_tiles, l):
    T, D = xs.shape
    tm = ROW_TILE
    row = lambda w: pl.BlockSpec((tm, w), lambda i: (i, 0))
    col = pl.BlockSpec((HEAD_LANES, tm), lambda i: (0, i))
    headsT = lambda nh, rows=HEAD_LANES: pl.BlockSpec((nh, rows, tm), lambda i: (0, 0, i))
    in_specs = [row(D), _mod_spec(l, n_lat_tiles), _layer_spec(gmix, l)]
    in_specs += [row(HEAD_LANES)] * 6 + [col] * 6 + [_layer_spec(w, l) for w in wts]
    out_shape = (
        jax.ShapeDtypeStruct((T, RET_HEADS * RET_DK), F32),
        jax.ShapeDtypeStruct((T, RET_HEADS * RET_DK), F32),
        jax.ShapeDtypeStruct((T, RET_W), BF16),
        jax.ShapeDtypeStruct((T, RET_W), F32),
        jax.ShapeDtypeStruct((MLA_HEADS, HEAD_LANES, T), BF16),
        jax.ShapeDtypeStruct((T, MLA_HEADS * HEAD_LANES), BF16),
        jax.ShapeDtypeStruct((MLA_HEADS, V_ROWS, T), BF16),
        jax.ShapeDtypeStruct((GQA_HEADS, HEAD_LANES, T), BF16),
        jax.ShapeDtypeStruct((T, GQA_KV_HEADS * GQA_HD), BF16),
        jax.ShapeDtypeStruct((GQA_KV_HEADS, V_ROWS, T), BF16),
    )
    out_specs = (
        row(RET_HEADS * RET_DK), row(RET_HEADS * RET_DK), row(RET_W), row(RET_W),
        headsT(MLA_HEADS), row(MLA_HEADS * HEAD_LANES), headsT(MLA_HEADS, V_ROWS),
        headsT(GQA_HEADS), row(GQA_KV_HEADS * GQA_HD), headsT(GQA_KV_HEADS, V_ROWS),
    )
    return pl.pallas_call(
        _inproj_kernel,
        out_shape=out_shape,
        grid=(T // tm,),
        in_specs=in_specs,
        out_specs=out_specs,
        compiler_params=_params("arbitrary"),
        name="inproj",
    )(xs, mods, gmix, *tabs, *tabsT, *wts)


RET_CHUNKS_PER_STEP = 2


def _ret_kernel(qf_ref, kf_ref, vf_ref, qb_ref, kb_ref, vb_ref,
                dec_ref, xi_ref, zeta_ref, cd_ref,
                yf_ref, yb_ref, sf_ref, sb_ref):
    @pl.when(pl.program_id(0) == 0)
    def _():
        sf_ref[...] = jnp.zeros_like(sf_ref)
        sb_ref[...] = jnp.zeros_like(sb_ref)

    C = RET_CHUNK
    dirs = ((qf_ref, kf_ref, vf_ref, yf_ref, sf_ref, range(RET_CHUNKS_PER_STEP)),
            (qb_ref, kb_ref, vb_ref, yb_ref, sb_ref, range(RET_CHUNKS_PER_STEP - 1, -1, -1)))
    states = [[s_ref[h] for h in range(RET_HEADS)] for *_, s_ref, _ in dirs]
    pending = []
    for pos in range(RET_CHUNKS_PER_STEP):
        for d, (q_ref, k_ref, v_ref, y_ref, s_ref, order) in enumerate(dirs):
            rows = pl.ds(order[pos] * C, C)
            for h in range(RET_HEADS):
                ksl = slice(h * RET_DK, (h + 1) * RET_DK)
                vsl = slice(h * RET_DV, (h + 1) * RET_DV)
                k = k_ref[rows, ksl]
                v = v_ref[rows, vsl]
                qb16 = q_ref[rows, ksl].astype(BF16)
                scores = lax.dot_general(qb16, k.astype(BF16), (((1,), (1,)), ((), ())),
                                         preferred_element_type=F32) * dec_ref[d, h]
                state = states[d][h]
                cross = _dot(qb16, state.astype(BF16)) * xi_ref[d, h]
                kz = (k * zeta_ref[d, h]).T.astype(BF16)
                states[d][h] = cd_ref[d, h] * state + _dot(kz, v)
                pending.append((y_ref, rows, vsl, scores.astype(BF16), v, cross))
    for d, (*_, s_ref, _) in enumerate(dirs):
        for h in range(RET_HEADS):
            s_ref[h] = states[d][h]
    for y_ref, rows, vsl, scores, v, cross in pending:
        y_ref[rows, vsl] = _dot(scores, v) + cross


def _ret_call(rq, rk, rv, consts, n_lat, n_ctx):
    T = rq.shape[0]
    rows = RET_CHUNK * RET_CHUNKS_PER_STEP
    assert n_lat % RET_CHUNKS_PER_STEP == 0 and n_ctx % RET_CHUNKS_PER_STEP == 0
    g_lat, g_ctx = n_lat // RET_CHUNKS_PER_STEP, n_ctx // RET_CHUNKS_PER_STEP
    n = g_lat + g_ctx

    def fwd(i):
        return jnp.where(i < g_ctx, g_lat + i, i - g_ctx)

    def bwd(i):
        return n - 1 - i

    kw, vw = RET_HEADS * RET_DK, RET_W
    spec = lambda w, order: pl.BlockSpec((rows, w), lambda i: (order(i), 0))
    dec, xi, zeta, cd = consts
    return pl.pallas_call(
        _ret_kernel,
        out_shape=(jax.ShapeDtypeStruct((T, vw), F32), jax.ShapeDtypeStruct((T, vw), F32)),
        grid=(n,),
        in_specs=[spec(kw, fwd), spec(kw, fwd), spec(vw, fwd),
                  spec(kw, bwd), spec(kw, bwd), spec(vw, bwd),
                  _const_spec(dec.shape), _const_spec(xi.shape),
                  _const_spec(zeta.shape), _const_spec(cd.shape)],
        out_specs=(spec(vw, fwd), spec(vw, bwd)),
        scratch_shapes=[pltpu.VMEM((RET_HEADS, RET_DK, RET_DV), F32),
                        pltpu.VMEM((RET_HEADS, RET_DK, RET_DV), F32)],
        compiler_params=_params("arbitrary"),
        name="retention",
    )(rq, rk, rv, rq, rk, rv, dec, xi, zeta, cd)


def _ret_consts():
    C = RET_CHUNK
    h = jnp.arange(RET_HEADS, dtype=F32)
    pos = jnp.arange(C, dtype=F32)
    diff = pos[:, None] - pos[None, :]

    def one(offset, reverse):
        lg = jnp.log1p(-jnp.exp2(-(RET_DECAY_START + offset) - h))
        dd = -diff if reverse else diff
        keep = dd >= 0
        dec = jnp.where(keep[None], jnp.exp(lg[:, None, None] * jnp.where(keep, dd, 0.0)[None]), 0.0)
        p = (C - 1 - pos) if reverse else pos
        xi = jnp.exp(lg[:, None] * (p + 1)[None, :])
        zeta = jnp.exp(lg[:, None] * (C - 1 - p)[None, :])
        cd = jnp.exp(lg * C)
        return dec, xi[:, :, None], zeta[:, :, None], jnp.broadcast_to(cd[:, None, None], (RET_HEADS, 1, 1))

    f = one(0.0, False)
    b = one(RET_BWD_OFFSET, True)
    return tuple(jnp.stack([a, c]) for a, c in zip(f, b))


def _chunk_colors(nk):
    n_main = (nk // 3) * 3
    if n_main == 0:
        return tuple(range(nk))
    return tuple(c % 3 if c < n_main else 3 + c - n_main for c in range(nk))


def _flash_kernel(qT_ref, k_ref, vT_ref, o_ref, m_ref, acc_ref, *buf_refs, tq, nq, tk, nk, sub):
    lag = FLASH_LAG
    colors = _chunk_colors(nk)
    n_buf = max(colors) + 1
    s_bufs, c_bufs = buf_refs[:n_buf], buf_refs[n_buf:]
    n_sub = tk // sub
    n_loop = max(((nk // 3) * 3 - lag) // 3, 0)

    def key_rows(c, r):
        base = c * tk if isinstance(c, int) else pl.multiple_of(c * tk, tk)
        return pl.ds(base + r * sub, sub)

    def q_cols(i):
        return pl.ds(i * tq if isinstance(i, int) else pl.multiple_of(i * tq, tq), tq)

    def tick(consume, score):
        if consume is not None:
            c, buf = consume
            m_prev = m_ref[...]
            m_new = jnp.maximum(m_prev, c_bufs[buf][...])
            alpha = jnp.exp2(m_prev - m_new)
            m_ref[...] = m_new
        cmax = pv = None
        for r in range(n_sub):
            rows = pl.ds(r * sub, sub)
            if consume is not None:
                p = jnp.exp2(s_bufs[buf][rows, :] - m_new).astype(BF16)
                part = _dot(vT_ref[0, :, key_rows(c, r)], p)
                pv = part if pv is None else pv + part
            if score is not None:
                s = _dot(k_ref[key_rows(score[0], r), :], score[2])
                s_bufs[score[1]][rows, :] = s
                part = jnp.max(s, axis=0, keepdims=True)
                cmax = part if cmax is None else jnp.maximum(cmax, part)
        if score is not None:
            c_bufs[score[1]][...] = cmax
        if consume is not None:
            acc_ref[...] = alpha * acc_ref[...] + pv

    def q_tile(i, carry, has_next=True):
        q = qT_ref[0, :, q_cols(i)]
        q_next = qT_ref[0, :, q_cols(i + 1)] if has_next else None
        m_ref[...] = jnp.full_like(m_ref, NEG_BIG)
        acc_ref[...] = jnp.zeros_like(acc_ref)

        def body(t, carry):
            for u in range(3):
                tick((t * 3 + u, u), (t * 3 + u + lag, (u + lag) % 3, q))
            return carry

        if n_loop:
            lax.fori_loop(0, n_loop, body, 0)
        for c in range(3 * n_loop, nk):
            ahead = c + lag
            if ahead < nk:
                score = (ahead, colors[ahead], q)
            elif has_next and ahead - nk < min(lag, nk):
                score = (ahead - nk, colors[ahead - nk], q_next)
            else:
                score = None
            tick((c, colors[c]), score)
        acc = acc_ref[...]
        o_ref[q_cols(i), :] = (acc[:HEAD_LANES] / acc[HEAD_LANES:HEAD_LANES + 1]).T.astype(o_ref.dtype)
        return carry

    q0 = qT_ref[0, :, q_cols(0)]
    for c in range(min(lag, nk)):
        tick(None, (c, colors[c], q0))
    if nq > 1:
        lax.fori_loop(0, nq - 1, q_tile, 0)
    q_tile(nq - 1, 0, has_next=False)


def _flash_kernel_into(qT_ref, k_ref, vT_ref, y_hbm_ref, o_ref, *scratch, **kw):
    del y_hbm_ref
    _flash_kernel(qT_ref, k_ref, vT_ref, o_ref, *scratch, **kw)


def _flash_call(qT, k, vT, y_prev, *, group, q_rows, q_off, kv_rows, kv_off, tq, tk, name):
    hq, T = qT.shape[0], qT.shape[2]
    sub = min(tk, FLASH_SUB)
    assert q_rows % tq == 0 and q_off % q_rows == 0 and kv_rows % tk == 0 and tk % sub == 0
    nq, nk = q_rows // tq, kv_rows // tk
    n_buf = max(_chunk_colors(nk)) + 1
    qb, kb = q_off // q_rows, kv_off // kv_rows
    kw = dict(tq=tq, nq=nq, tk=tk, nk=nk, sub=sub)
    in_specs = [
        pl.BlockSpec((1, HEAD_LANES, q_rows), lambda h: (h, 0, qb), pipeline_mode=pl.Buffered(1)),
        pl.BlockSpec((kv_rows, HEAD_LANES), lambda h: (kb, h // group)),
        pl.BlockSpec((1, V_ROWS, kv_rows), lambda h: (h // group, 0, kb)),
    ]
    operands = (qT, k, vT)
    if y_prev is not None:
        in_specs.append(pl.BlockSpec(memory_space=pl.ANY))
        operands += (y_prev,)
    return pl.pallas_call(
        functools.partial(_flash_kernel if y_prev is None else _flash_kernel_into, **kw),
        out_shape=jax.ShapeDtypeStruct((T, hq * HEAD_LANES), BF16),
        grid=(hq,),
        in_specs=in_specs,
        out_specs=pl.BlockSpec((q_rows, HEAD_LANES), lambda h: (qb, h)),
        scratch_shapes=[pltpu.VMEM((1, tq), F32), pltpu.VMEM((V_ROWS, tq), F32)]
                       + [pltpu.VMEM((tk, tq), F32)] * n_buf + [pltpu.VMEM((1, tq), F32)] * n_buf,
        input_output_aliases={} if y_prev is None else {3: 0},
        compiler_params=_params("arbitrary"),
        name=name,
    )(*operands)


def _pick_tile(n, candidates):
    for t in candidates:
        if n % t == 0:
            return t
    raise ValueError(f"no tile for {n}")


def _attention(qT, k, vT, group, L, Cn, name):
    T = L + Cn
    tq = _pick_tile(L, (1024, 512, 256, 128))
    tk = _pick_tile(T, (1280, 256, 128))
    y = _flash_call(qT, k, vT, None, group=group, q_rows=L, q_off=0, kv_rows=T, kv_off=0,
                    tq=tq, tk=tk, name=name)
    return _flash_call(qT, k, vT, y, group=group, q_rows=Cn, q_off=L, kv_rows=Cn, kv_off=L,
                       tq=Cn, tk=Cn, name=name + "_ctx")


def _mix_ffn_kernel(x_ref, mod_ref, gmix_ref, yf_ref, yb_ref, rg_ref, ym_ref, yg_ref,
                    w_gates, w_ro, w_mo, w_go, w_out, gffn_ref, w_a, w_b, w_o, gf_ref, o_ref,
                    *, final_norm):
    x = x_ref[...]
    mod = mod_ref[0]
    hb = _prenorm(x, mod, gmix_ref[...], 0, 1).astype(BF16)
    gs = jax.nn.sigmoid(_dot(hb, w_gates[...]))
    y = yf_ref[...] + yb_ref[...]
    g = _silu(rg_ref[...])
    parts = []
    for h in range(RET_HEADS):
        sl = slice(h * RET_DV, (h + 1) * RET_DV)
        parts.append((g[:, sl] * _rms(y[:, sl])).astype(BF16))
    y_ret = jnp.concatenate(parts, axis=-1)
    D = D_MODEL
    z = (gs[:, :D] * _dot(y_ret, w_ro[...])
         + gs[:, D:2 * D] * _dot(ym_ref[...], w_mo[...])
         + gs[:, 2 * D:] * _dot(yg_ref[...], w_go[...]))
    x = x + mod[2:3] * _dot(z.astype(BF16), w_out[...])

    hb = _prenorm(x, mod, gffn_ref[...], 3, 4).astype(BF16)
    u = (_silu(_dot(hb, w_a[...])) * _dot(hb, w_b[...])).astype(BF16)
    x = x + mod[5:6] * _dot(u, w_o[...])
    o_ref[...] = _rms(x) * gf_ref[...] if final_norm else x


def _mix_ffn_call(xs, mods, gmix, yf, yb, rg, ym, yg, merge_w, g_ffn, ffn_w, g_final, n_lat_tiles, l,
                  rows, final_norm):
    D = xs.shape[1]
    tm = ROW_TILE
    row = lambda w: pl.BlockSpec((tm, w), lambda i: (i, 0))
    once = lambda w: _layer_spec(w, l, pl.Buffered(1))
    return pl.pallas_call(
        functools.partial(_mix_ffn_kernel, final_norm=final_norm),
        out_shape=jax.ShapeDtypeStruct((rows, D), F32),
        grid=(rows // tm,),
        in_specs=[row(D), _mod_spec(l, n_lat_tiles), _layer_spec(gmix, l),
                  row(RET_W), row(RET_W), row(RET_W), row(MLA_W), row(GQA_W)]
                 + [once(w) for w in merge_w] + [_layer_spec(g_ffn, l)] + [once(w) for w in ffn_w]
                 + [_const_spec((1, D))],
        out_specs=row(D),
        compiler_params=_params("arbitrary"),
        name="mix_ffn",
    )(xs, mods, gmix, yf, yb, rg, ym, yg, *merge_w, g_ffn, *ffn_w, g_final)


def _rope_tables(L, Cn):
    n_rows = L // GRID_W

    def angles(n, dim):
        half = dim // 2
        inv_freq = ROPE_BASE ** (-jnp.arange(half, dtype=F32) / half)
        ang = jnp.arange(n, dtype=jnp.int32).astype(F32)[:, None] * inv_freq[None, :]
        return jnp.cos(ang), jnp.sin(ang)

    def tables(dim, lane0):
        cr, sr = (jnp.repeat(a, GRID_W, axis=0) for a in angles(n_rows, dim // 2))
        cc, sc = (jnp.tile(a, (n_rows, 1)) for a in angles(GRID_W, dim // 2))
        z = jnp.zeros_like(sr)
        c = jnp.concatenate([cr, cr, cc, cc], axis=1)
        sa = jnp.concatenate([-sr, z, -sc, z], axis=1)
        sb = jnp.concatenate([z, sr, z, sc], axis=1)
        pad = lambda a, fill: jnp.concatenate(
            [jnp.full((L, lane0), fill, F32), a, jnp.full((L, HEAD_LANES - lane0 - dim), fill, F32)], axis=1)
        c, sa, sb = pad(c, 1.0), pad(sa, 0.0), pad(sb, 0.0)
        ctx = lambda fill: jnp.full((Cn, HEAD_LANES), fill, F32)
        return (jnp.concatenate([c, ctx(1.0)]), jnp.concatenate([sa, ctx(0.0)]),
                jnp.concatenate([sb, ctx(0.0)]))

    return tables(RET_DK, 0) + tables(MLA_ROPE, MLA_NOPE)


def _stacked_weights(w_in, w_mla_qb, w_mla_kvb):
    depth = w_in.shape[0]
    split_at = np.cumsum(IN_SIZES)[:-1].tolist()
    rq, rk, rv, rg, cq, ckv, kr, gq, gk, gv, gates = (
        w.astype(BF16) for w in jnp.split(w_in, split_at, axis=-1))
    kr_pad = jnp.pad(kr, ((0, 0), (0, 0), (MLA_NOPE, HEAD_LANES - MLA_NOPE - MLA_ROPE)))
    qb = w_mla_qb.astype(BF16).reshape(depth, MLA_Q_RANK, MLA_HEADS, MLA_NOPE + MLA_ROPE)
    qb = jnp.pad(qb, ((0, 0), (0, 0), (0, 0), (0, HEAD_LANES - MLA_NOPE - MLA_ROPE)))
    kvb = w_mla_kvb.astype(BF16).reshape(depth, MLA_KV_RANK, MLA_HEADS, MLA_NOPE + MLA_V)
    kb = jnp.pad(kvb[..., :MLA_NOPE], ((0, 0), (0, 0), (0, 0), (0, HEAD_LANES - MLA_NOPE)))
    flat = lambda w: w.reshape(depth, w.shape[1], -1)
    wT = lambda w: jnp.swapaxes(w, 1, 2)
    natural = (rq, rk, rv, rg, ckv, kr_pad, gk, flat(kb))
    transposed = (wT(cq), wT(flat(qb)), wT(flat(kvb[..., MLA_NOPE:])), wT(gq), wT(gv))
    return natural, transposed, gates


def kernel(x, c, ctx, c_ctx, w_mod, b_mod, g_mix, w_in, g_mla_q, g_mla_kv, w_mla_qb, w_mla_kvb,
           g_gqa_q, g_gqa_k, w_ret_o, w_mla_o, w_gqa_o, w_out, g_ffn, w_ffn_in, w_ffn_out, g_final):
    B, L, D = x.shape
    Cn = ctx.shape[1]
    depth = w_mod.shape[0]
    assert B == 1 and D == D_MODEL and L % ROW_TILE == 0 and Cn % ROW_TILE == 0 and L % Cn == 0
    assert L % GRID_W == 0
    n_lat_tiles = L // ROW_TILE
    ffn_hidden = w_ffn_out.shape[1]

    xs = jnp.concatenate([x[0], ctx[0]], axis=0)
    cond = jnp.zeros((MOD_ROWS, D), F32).at[0].set(c[0]).at[1].set(c_ctx)
    mods = _mod_call(cond, w_mod, b_mod)
    mods = mods[:, :2].reshape(depth, 2, N_MOD, D)
    mods = jnp.pad(mods, ((0, 0), (0, 0), (0, MOD_ROWS - N_MOD), (0, 0)))

    tabs = _rope_tables(L, Cn)
    ret_consts = _ret_consts()
    rows3 = lambda g: g.reshape(depth, 1, -1)
    w_nat, w_tr, w_gates = _stacked_weights(w_in, w_mla_qb, w_mla_kvb)
    lanes = lambda g: jnp.broadcast_to(g[:, :, None], g.shape + (ROW_TILE,))
    wts = w_nat + w_tr + (rows3(g_mla_kv), rows3(g_gqa_k), lanes(g_mla_q), lanes(g_gqa_q))
    tabsT = tuple(t.T for t in tabs)
    merge_w = (w_gates,) + tuple(w.astype(BF16) for w in (w_ret_o, w_mla_o, w_gqa_o, w_out))
    w_ffn_a = w_ffn_in[..., :ffn_hidden].astype(BF16)
    w_ffn_b = w_ffn_in[..., ffn_hidden:].astype(BF16)
    w_ffn_o = w_ffn_out.astype(BF16)
    g_mix3, g_ffn3 = rows3(g_mix), rows3(g_ffn)

    for l in range(depth):
        rq, rk, rv, rg, mqT, mk, mvT, gqT, gk, gvT = _inproj_call(
            xs, mods, g_mix3, tabs, tabsT, wts, n_lat_tiles, l)
        yf, yb = _ret_call(rq, rk, rv, ret_consts, L // RET_CHUNK, Cn // RET_CHUNK)
        ym = _attention(mqT, mk, mvT, 1, L, Cn, "mla")
        yg = _attention(gqT, gk, gvT, GQA_HEADS // GQA_KV_HEADS, L, Cn, "gqa")
        last = l == depth - 1
        xs = _mix_ffn_call(xs, mods, g_mix3, yf, yb, rg, ym, yg, merge_w, g_ffn3,
                           (w_ffn_a, w_ffn_b, w_ffn_o), g_final.reshape(1, -1), n_lat_tiles, l,
                           rows=L if last else L + Cn, final_norm=last)
    return xs[None]
```

```python
import functools
import math

import jax
import jax.numpy as jnp
import numpy as np
from jax import lax
from jax.experimental import pallas as pl
from jax.experimental.pallas import tpu as pltpu

D_MODEL = 1024
GRID_W = 64
RET_CHUNK = 128
ROPE_BASE = 10000.0
NORM_EPS = 1e-6
N_MOD = 6

RET_HEADS = 4
RET_DK = 128
RET_DV = 256
RET_DECAY_START = 5.0
RET_BWD_OFFSET = 0.5

MLA_HEADS = 8
MLA_Q_RANK = 256
MLA_KV_RANK = 256
MLA_NOPE = 64
MLA_ROPE = 32
MLA_V = 128

GQA_HEADS = 8
GQA_KV_HEADS = 2
GQA_HD = 128

RET_W = RET_HEADS * RET_DV
MLA_W = MLA_HEADS * MLA_V
GQA_W = GQA_HEADS * GQA_HD
N_BRANCH = 3

IN_SIZES = (
    RET_HEADS * RET_DK, RET_HEADS * RET_DK, RET_W, RET_W,
    MLA_Q_RANK, MLA_KV_RANK, MLA_ROPE,
    GQA_W, GQA_KV_HEADS * GQA_HD, GQA_KV_HEADS * GQA_HD,
    N_BRANCH * D_MODEL,
)

HEAD_LANES = 128
BF16_SUBLANES = 16
V_ROWS = HEAD_LANES + BF16_SUBLANES
MOD_ROWS = 8
ROW_TILE = 256
FLASH_LAG = 2
FLASH_SUB = 256
V7X_VMEM_LIMIT = 56 * 1024 * 1024
LOG2E = math.log2(math.e)
NEG_BIG = -1e30

F32 = jnp.float32
BF16 = jnp.bfloat16


def _params(*sem):
    return pltpu.CompilerParams(dimension_semantics=sem, vmem_limit_bytes=V7X_VMEM_LIMIT)


def _const_spec(shape):
    zeros = (0,) * len(shape)
    return pl.BlockSpec(shape, lambda *_: zeros)


def _layer_spec(w, l, pipeline_mode=None):
    zeros = (0,) * (w.ndim - 1)
    return pl.BlockSpec((None,) + w.shape[1:], lambda *_: (l,) + zeros, pipeline_mode=pipeline_mode)


def _mod_spec(l, n_lat_tiles):
    return pl.BlockSpec((None, 1, MOD_ROWS, D_MODEL), lambda i: (l, i // n_lat_tiles, 0, 0))


def _dot(a, b):
    return jnp.dot(a, b, preferred_element_type=F32)


def _rms(x):
    return x * lax.rsqrt(jnp.mean(x * x, axis=-1, keepdims=True) + NORM_EPS)


def _silu(x):
    return x * jax.nn.sigmoid(x)


def _rope(x, c, sa, sb, block):
    up = pltpu.roll(x, HEAD_LANES - block, 1)
    dn = pltpu.roll(x, block, 1)
    return x * c + up * sa + dn * sb


def _head(x, h):
    return x[:, h * HEAD_LANES:(h + 1) * HEAD_LANES]


def _mod_kernel(cond_ref, w_ref, b_ref, o_ref):
    s = _silu(cond_ref[...]).astype(BF16)
    o_ref[0] = _dot(s, w_ref[0].astype(BF16)) + b_ref[0]


def _mod_call(cond, w_mod, b_mod):
    depth, d, n = w_mod.shape
    tn = n // 4
    return pl.pallas_call(
        _mod_kernel,
        out_shape=jax.ShapeDtypeStruct((depth, MOD_ROWS, n), F32),
        grid=(depth, n // tn),
        in_specs=[
            pl.BlockSpec((MOD_ROWS, d), lambda l, j: (0, 0)),
            pl.BlockSpec((1, d, tn), lambda l, j: (l, 0, j)),
            pl.BlockSpec((1, 1, tn), lambda l, j: (l, 0, j)),
        ],
        out_specs=pl.BlockSpec((1, MOD_ROWS, tn), lambda l, j: (l, 0, j)),
        compiler_params=_params("arbitrary", "arbitrary"),
        name="mod",
    )(cond, w_mod, b_mod.reshape(depth, 1, n))


def _prenorm(x, mod, g, shift_row, scale_row):
    h = _rms(x) * g
    return h * (1.0 + mod[scale_row:scale_row + 1]) + mod[shift_row:shift_row + 1]


def _rms_rows(x):
    return x * lax.rsqrt(jnp.mean(x * x, axis=0, keepdims=True) + NORM_EPS)


def _rope_rows(x, c, sa, sb, block):
    up = jnp.concatenate([x[block:], x[:block]], axis=0)
    dn = jnp.concatenate([x[-block:], x[:-block]], axis=0)
    return x * c + up * sa + dn * sb


def _inproj_kernel(x_ref, mod_ref, gmix_ref, c_ref, sa_ref, sb_ref, cm_ref, sam_ref, sbm_ref,
                   cT_ref, saT_ref, sbT_ref, cmT_ref, samT_ref, sbmT_ref,
                   w_rq, w_rk, w_rv, w_rg, w_ckv, w_kr, w_gk, w_kb,
                   w_cqT, w_qbT, w_vbT, w_gqT, w_gvT, g_mkv, g_gk, g_mqT, g_gqT,
                   rq_o, rk_o, rv_o, rg_o, mqT_o, mk_o, mvT_o, gqT_o, gk_o, gvT_o):
    h = _prenorm(x_ref[...], mod_ref[0], gmix_ref[...], 0, 1)
    hb = h.astype(BF16)
    hTb = h.T.astype(BF16)
    c, sa, sb = c_ref[...], sa_ref[...], sb_ref[...]
    cm, sam, sbm = cm_ref[...], sam_ref[...], sbm_ref[...]
    rope_big = lambda v: _rope(v, c, sa, sb, RET_DK // 4)
    rope_mla = lambda v: _rope(v, cm, sam, sbm, MLA_ROPE // 4)
    rows = lambda v, i: v[i * HEAD_LANES:(i + 1) * HEAD_LANES]
    ones_rows = (lax.broadcasted_iota(jnp.int32, (BF16_SUBLANES, x_ref.shape[0]), 0) == 0).astype(BF16)

    cqnT = (_rms_rows(_dot(w_cqT[...], hTb)) * g_mqT[...]).astype(BF16)
    ckvn = _rms(_dot(hb, w_ckv[...])) * g_mkv[...]

    rq = _dot(hb, w_rq[...])
    rk = _dot(hb, w_rk[...])
    for i in range(RET_HEADS):
        sl = slice(i * HEAD_LANES, (i + 1) * HEAD_LANES)
        rq_o[:, sl] = rope_big(_head(rq, i))
        rk_o[:, sl] = rope_big(_head(rk, i)) * (RET_DK ** -0.5)

    gqa_qscale = GQA_HD ** -0.5 * LOG2E
    gqT = _dot(w_gqT[...], hTb)
    cT, saT, sbT = cT_ref[...], saT_ref[...], sbT_ref[...]
    for i in range(GQA_HEADS):
        qn = _rms_rows(rows(gqT, i)) * g_gqT[...]
        gqT_o[i] = (_rope_rows(qn, cT, saT, sbT, GQA_HD // 4) * gqa_qscale).astype(BF16)
    gk = _dot(hb, w_gk[...])
    gvT = _dot(w_gvT[...], hTb)
    for i in range(GQA_KV_HEADS):
        sl = slice(i * HEAD_LANES, (i + 1) * HEAD_LANES)
        gk_o[:, sl] = rope_big(_rms(_head(gk, i)) * g_gk[...]).astype(BF16)
        gvT_o[i, :HEAD_LANES, :] = rows(gvT, i).astype(BF16)
        gvT_o[i, HEAD_LANES:, :] = ones_rows

    rv_o[...] = _dot(hb, w_rv[...]).astype(BF16)
    rg_o[...] = _dot(hb, w_rg[...])

    mla_qscale = (MLA_NOPE + MLA_ROPE) ** -0.5 * LOG2E
    kr = rope_mla(_dot(hb, w_kr[...]))
    qT = _dot(w_qbT[...], cqnT)
    kn = _dot(ckvn.astype(BF16), w_kb[...])
    vT = _dot(w_vbT[...], ckvn.T.astype(BF16))
    cmT, samT, sbmT = cmT_ref[...], samT_ref[...], sbmT_ref[...]
    for i in range(MLA_HEADS):
        sl = slice(i * HEAD_LANES, (i + 1) * HEAD_LANES)
        mqT_o[i] = (_rope_rows(rows(qT, i), cmT, samT, sbmT, MLA_ROPE // 4) * mla_qscale).astype(BF16)
        mk_o[:, sl] = (_head(kn, i) + kr).astype(BF16)
        mvT_o[i, :HEAD_LANES, :] = rows(vT, i).astype(BF16)
        mvT_o[i, HEAD_LANES:, :] = ones_rows


def _inproj_call(xs, mods, gmix, tabs, tabsT, wts, n_lat_tiles, l):
    T, D = xs.shape
    tm = ROW_TILE
    row = lambda w: pl.BlockSpec((tm, w), lambda i: (i, 0))
    col = pl.BlockSpec((HEAD_LANES, tm), lambda i: (0, i))
    headsT = lambda nh, rows=HEAD_LANES: pl.BlockSpec((nh, rows, tm), lambda i: (0, 0, i))
    in_specs = [row(D), _mod_spec(l, n_lat_tiles), _layer_spec(gmix, l)]
    in_specs += [row(HEAD_LANES)] * 6 + [col] * 6 + [_layer_spec(w, l) for w in wts]
    out_shape = (
        jax.ShapeDtypeStruct((T, RET_HEADS * RET_DK), F32),
        jax.ShapeDtypeStruct((T, RET_HEADS * RET_DK), F32),
        jax.ShapeDtypeStruct((T, RET_W), BF16),
        jax.ShapeDtypeStruct((T, RET_W), F32),
        jax.ShapeDtypeStruct((MLA_HEADS, HEAD_LANES, T), BF16),
        jax.ShapeDtypeStruct((T, MLA_HEADS * HEAD_LANES), BF16),
        jax.ShapeDtypeStruct((MLA_HEADS, V_ROWS, T), BF16),
        jax.ShapeDtypeStruct((GQA_HEADS, HEAD_LANES, T), BF16),
        jax.ShapeDtypeStruct((T, GQA_KV_HEADS * GQA_HD), BF16),
        jax.ShapeDtypeStruct((GQA_KV_HEADS, V_ROWS, T), BF16),
    )
    out_specs = (
        row(RET_HEADS * RET_DK), row(RET_HEADS * RET_DK), row(RET_W), row(RET_W),
        headsT(MLA_HEADS), row(MLA_HEADS * HEAD_LANES), headsT(MLA_HEADS, V_ROWS),
        headsT(GQA_HEADS), row(GQA_KV_HEADS * GQA_HD), headsT(GQA_KV_HEADS, V_ROWS),
    )
    return pl.pallas_call(
        _inproj_kernel,
        out_shape=out_shape,
        grid=(T // tm,),
        in_specs=in_specs,
        out_specs=out_specs,
        compiler_params=_params("arbitrary"),
        name="inproj",
    )(xs, mods, gmix, *tabs, *tabsT, *wts)


RET_CHUNKS_PER_STEP = 2


def _ret_kernel(qf_ref, kf_ref, vf_ref, qb_ref, kb_ref, vb_ref,
                dec_ref, xi_ref, zeta_ref, cd_ref,
                yf_ref, yb_ref, sf_ref, sb_ref):
    @pl.when(pl.program_id(0) == 0)
    def _():
        sf_ref[...] = jnp.zeros_like(sf_ref)
        sb_ref[...] = jnp.zeros_like(sb_ref)

    C = RET_CHUNK
    dirs = ((qf_ref, kf_ref, vf_ref, yf_ref, sf_ref, range(RET_CHUNKS_PER_STEP)),
            (qb_ref, kb_ref, vb_ref, yb_ref, sb_ref, range(RET_CHUNKS_PER_STEP - 1, -1, -1)))
    states = [[s_ref[h] for h in range(RET_HEADS)] for *_, s_ref, _ in dirs]
    pending = []
    for pos in range(RET_CHUNKS_PER_STEP):
        for d, (q_ref, k_ref, v_ref, y_ref, s_ref, order) in enumerate(dirs):
            rows = pl.ds(order[pos] * C, C)
            for h in range(RET_HEADS):
                ksl = slice(h * RET_DK, (h + 1) * RET_DK)
                vsl = slice(h * RET_DV, (h + 1) * RET_DV)
                k = k_ref[rows, ksl]
                v = v_ref[rows, vsl]
                qb16 = q_ref[rows, ksl].astype(BF16)
                scores = lax.dot_general(qb16, k.astype(BF16), (((1,), (1,)), ((), ())),
                                         preferred_element_type=F32) * dec_ref[d, h]
                state = states[d][h]
                cross = _dot(qb16, state.astype(BF16)) * xi_ref[d, h]
                kz = (k * zeta_ref[d, h]).T.astype(BF16)
                states[d][h] = cd_ref[d, h] * state + _dot(kz, v)
                pending.append((y_ref, rows, vsl, scores.astype(BF16), v, cross))
    for d, (*_, s_ref, _) in enumerate(dirs):
        for h in range(RET_HEADS):
            s_ref[h] = states[d][h]
    for y_ref, rows, vsl, scores, v, cross in pending:
        y_ref[rows, vsl] = _dot(scores, v) + cross


def _ret_call(rq, rk, rv, consts, n_lat, n_ctx):
    T = rq.shape[0]
    rows = RET_CHUNK * RET_CHUNKS_PER_STEP
    assert n_lat % RET_CHUNKS_PER_STEP == 0 and n_ctx % RET_CHUNKS_PER_STEP == 0
    g_lat, g_ctx = n_lat // RET_CHUNKS_PER_STEP, n_ctx // RET_CHUNKS_PER_STEP
    n = g_lat + g_ctx

    def fwd(i):
        return jnp.where(i < g_ctx, g_lat + i, i - g_ctx)

    def bwd(i):
        return n - 1 - i

    kw, vw = RET_HEADS * RET_DK, RET_W
    spec = lambda w, order: pl.BlockSpec((rows, w), lambda i: (order(i), 0))
    dec, xi, zeta, cd = consts
    return pl.pallas_call(
        _ret_kernel,
        out_shape=(jax.ShapeDtypeStruct((T, vw), F32), jax.ShapeDtypeStruct((T, vw), F32)),
        grid=(n,),
        in_specs=[spec(kw, fwd), spec(kw, fwd), spec(vw, fwd),
                  spec(kw, bwd), spec(kw, bwd), spec(vw, bwd),
                  _const_spec(dec.shape), _const_spec(xi.shape),
                  _const_spec(zeta.shape), _const_spec(cd.shape)],
        out_specs=(spec(vw, fwd), spec(vw, bwd)),
        scratch_shapes=[pltpu.VMEM((RET_HEADS, RET_DK, RET_DV), F32),
                        pltpu.VMEM((RET_HEADS, RET_DK, RET_DV), F32)],
        compiler_params=_params("arbitrary"),
        name="retention",
    )(rq, rk, rv, rq, rk, rv, dec, xi, zeta, cd)


def _ret_consts():
    C = RET_CHUNK
    h = jnp.arange(RET_HEADS, dtype=F32)
    pos = jnp.arange(C, dtype=F32)
    diff = pos[:, None] - pos[None, :]

    def one(offset, reverse):
        lg = jnp.log1p(-jnp.exp2(-(RET_DECAY_START + offset) - h))
        dd = -diff if reverse else diff
        keep = dd >= 0
        dec = jnp.where(keep[None], jnp.exp(lg[:, None, None] * jnp.where(keep, dd, 0.0)[None]), 0.0)
        p = (C - 1 - pos) if reverse else pos
        xi = jnp.exp(lg[:, None] * (p + 1)[None, :])
        zeta = jnp.exp(lg[:, None] * (C - 1 - p)[None, :])
        cd = jnp.exp(lg * C)
        return dec, xi[:, :, None], zeta[:, :, None], jnp.broadcast_to(cd[:, None, None], (RET_HEADS, 1, 1))

    f = one(0.0, False)
    b = one(RET_BWD_OFFSET, True)
    return tuple(jnp.stack([a, c]) for a, c in zip(f, b))


def _chunk_colors(nk):
    n_main = (nk // 3) * 3
    if n_main == 0:
        return tuple(range(nk))
    return tuple(c % 3 if c < n_main else 3 + c - n_main for c in range(nk))


def _flash_kernel(qT_ref, k_ref, vT_ref, o_ref, m_ref, acc_ref, *buf_refs, tq, nq, tk, nk, sub, n_ctx):
    lag = FLASH_LAG
    colors = _chunk_colors(nk)
    n_buf = max(colors) + 1
    s_bufs, c_bufs = buf_refs[:n_buf], buf_refs[n_buf:]
    n_sub = tk // sub
    n_loop = max(((nk // 3) * 3 - lag) // 3, 0)

    def key_rows(c, r):
        base = c * tk if isinstance(c, int) else pl.multiple_of(c * tk, tk)
        return pl.ds(base + r * sub, sub)

    def q_cols(i):
        return pl.ds(i * tq if isinstance(i, int) else pl.multiple_of(i * tq, tq), tq)

    def tick(consume, score):
        if consume is not None:
            c, buf = consume
            m_prev = m_ref[...]
            m_new = jnp.maximum(m_prev, c_bufs[buf][...])
            alpha = jnp.exp2(m_prev - m_new)
            m_ref[...] = m_new
        cmax = pv = None
        for r in range(n_sub):
            rows = pl.ds(r * sub, sub)
            if consume is not None:
                p = jnp.exp2(s_bufs[buf][rows, :] - m_new).astype(BF16)
                part = _dot(vT_ref[0, :, key_rows(c, r)], p)
                pv = part if pv is None else pv + part
            if score is not None:
                s = _dot(k_ref[key_rows(score[0], r), :], score[2])
                s_bufs[score[1]][rows, :] = s
                part = jnp.max(s, axis=0, keepdims=True)
                cmax = part if cmax is None else jnp.maximum(cmax, part)
        if score is not None:
            c_bufs[score[1]][...] = cmax
        if consume is not None:
            acc_ref[...] = alpha * acc_ref[...] + pv

    def q_tile(i, carry, has_next=True):
        q = qT_ref[0, :, q_cols(i)]
        q_next = qT_ref[0, :, q_cols(i + 1)] if has_next else None
        m_ref[...] = jnp.full_like(m_ref, NEG_BIG)
        acc_ref[...] = jnp.zeros_like(acc_ref)

        def body(t, carry):
            for u in range(3):
                tick((t * 3 + u, u), (t * 3 + u + lag, (u + lag) % 3, q))
            return carry

        if n_loop:
            lax.fori_loop(0, n_loop, body, 0)
        for c in range(3 * n_loop, nk):
            ahead = c + lag
            if ahead < nk:
                score = (ahead, colors[ahead], q)
            elif has_next and ahead - nk < min(lag, nk):
                score = (ahead - nk, colors[ahead - nk], q_next)
            else:
                score = None
            tick((c, colors[c]), score)
        acc = acc_ref[...]
        o_ref[q_cols(i), :] = (acc[:HEAD_LANES] / acc[HEAD_LANES:HEAD_LANES + 1]).T.astype(o_ref.dtype)
        return carry

    q0 = qT_ref[0, :, q_cols(0)]
    for c in range(min(lag, nk)):
        tick(None, (c, colors[c], q0))
    if nq > 1:
        lax.fori_loop(0, nq - 1, q_tile, 0)
    q_tile(nq - 1, 0, has_next=False)

    ctx = pl.ds(nq * tq, n_ctx)
    s = _dot(k_ref[ctx, :], qT_ref[0, :, ctx])
    p = jnp.exp2(s - jnp.max(s, axis=0, keepdims=True)).astype(BF16)
    pv = _dot(vT_ref[0, :, ctx], p)
    o_ref[ctx, :] = (pv[:HEAD_LANES] / pv[HEAD_LANES:HEAD_LANES + 1]).T.astype(o_ref.dtype)


def _pick_tile(n, candidates):
    for t in candidates:
        if n % t == 0:
            return t
    raise ValueError(f"no tile for {n}")


def _attention(qT, k, vT, group, L, Cn, name):
    hq, T = qT.shape[0], qT.shape[2]
    tq = _pick_tile(L, (1024, 512, 256, 128))
    tk = _pick_tile(T, (1280, 256, 128))
    sub = min(tk, FLASH_SUB)
    assert T == L + Cn and tk % sub == 0 and Cn % HEAD_LANES == 0
    nq, nk = L // tq, T // tk
    n_buf = max(_chunk_colors(nk)) + 1
    return pl.pallas_call(
        functools.partial(_flash_kernel, tq=tq, nq=nq, tk=tk, nk=nk, sub=sub, n_ctx=Cn),
        out_shape=jax.ShapeDtypeStruct((T, hq * HEAD_LANES), BF16),
        grid=(hq,),
        in_specs=[
            pl.BlockSpec((1, HEAD_LANES, T), lambda h: (h, 0, 0), pipeline_mode=pl.Buffered(1)),
            pl.BlockSpec((T, HEAD_LANES), lambda h: (0, h // group)),
            pl.BlockSpec((1, V_ROWS, T), lambda h: (h // group, 0, 0)),
        ],
        out_specs=pl.BlockSpec((T, HEAD_LANES), lambda h: (0, h)),
        scratch_shapes=[pltpu.VMEM((1, tq), F32), pltpu.VMEM((V_ROWS, tq), F32)]
                       + [pltpu.VMEM((tk, tq), F32)] * n_buf + [pltpu.VMEM((1, tq), F32)] * n_buf,
        compiler_params=_params("arbitrary"),
        name=name,
    )(qT, k, vT)


def _mix_ffn_kernel(x_ref, mod_ref, gmix_ref, yf_ref, yb_ref, rg_ref, ym_ref, yg_ref,
                    w_gates, w_ro, w_mo, w_go, w_out, gffn_ref, w_a, w_b, w_o, gf_ref, o_ref,
                    *, final_norm):
    x = x_ref[...]
    mod = mod_ref[0]
    hb = _prenorm(x, mod, gmix_ref[...], 0, 1).astype(BF16)
    gs = jax.nn.sigmoid(_dot(hb, w_gates[...]))
    y = yf_ref[...] + yb_ref[...]
    g = _silu(rg_ref[...])
    parts = []
    for h in range(RET_HEADS):
        sl = slice(h * RET_DV, (h + 1) * RET_DV)
        parts.append((g[:, sl] * _rms(y[:, sl])).astype(BF16))
    y_ret = jnp.concatenate(parts, axis=-1)
    D = D_MODEL
    z = (gs[:, :D] * _dot(y_ret, w_ro[...])
         + gs[:, D:2 * D] * _dot(ym_ref[...], w_mo[...])
         + gs[:, 2 * D:] * _dot(yg_ref[...], w_go[...]))
    x = x + mod[2:3] * _dot(z.astype(BF16), w_out[...])

    hb = _prenorm(x, mod, gffn_ref[...], 3, 4).astype(BF16)
    u = (_silu(_dot(hb, w_a[...])) * _dot(hb, w_b[...])).astype(BF16)
    x = x + mod[5:6] * _dot(u, w_o[...])
    o_ref[...] = _rms(x) * gf_ref[...] if final_norm else x


def _mix_ffn_call(xs, mods, gmix, yf, yb, rg, ym, yg, merge_w, g_ffn, ffn_w, g_final, n_lat_tiles, l,
                  rows, final_norm):
    D = xs.shape[1]
    tm = ROW_TILE
    row = lambda w: pl.BlockSpec((tm, w), lambda i: (i, 0))
    once = lambda w: _layer_spec(w, l, pl.Buffered(1))
    return pl.pallas_call(
        functools.partial(_mix_ffn_kernel, final_norm=final_norm),
        out_shape=jax.ShapeDtypeStruct((rows, D), F32),
        grid=(rows // tm,),
        in_specs=[row(D), _mod_spec(l, n_lat_tiles), _layer_spec(gmix, l),
                  row(RET_W), row(RET_W), row(RET_W), row(MLA_W), row(GQA_W)]
                 + [once(w) for w in merge_w] + [_layer_spec(g_ffn, l)] + [once(w) for w in ffn_w]
                 + [_const_spec((1, D))],
        out_specs=row(D),
        compiler_params=_params("arbitrary"),
        name="mix_ffn",
    )(xs, mods, gmix, yf, yb, rg, ym, yg, *merge_w, g_ffn, *ffn_w, g_final)


def _rope_tables(L, Cn):
    n_rows = L // GRID_W

    def angles(n, dim):
        half = dim // 2
        inv_freq = ROPE_BASE ** (-jnp.arange(half, dtype=F32) / half)
        ang = jnp.arange(n, dtype=jnp.int32).astype(F32)[:, None] * inv_freq[None, :]
        return jnp.cos(ang), jnp.sin(ang)

    def tables(dim, lane0):
        cr, sr = (jnp.repeat(a, GRID_W, axis=0) for a in angles(n_rows, dim // 2))
        cc, sc = (jnp.tile(a, (n_rows, 1)) for a in angles(GRID_W, dim // 2))
        z = jnp.zeros_like(sr)
        c = jnp.concatenate([cr, cr, cc, cc], axis=1)
        sa = jnp.concatenate([-sr, z, -sc, z], axis=1)
        sb = jnp.concatenate([z, sr, z, sc], axis=1)
        pad = lambda a, fill: jnp.concatenate(
            [jnp.full((L, lane0), fill, F32), a, jnp.full((L, HEAD_LANES - lane0 - dim), fill, F32)], axis=1)
        c, sa, sb = pad(c, 1.0), pad(sa, 0.0), pad(sb, 0.0)
        ctx = lambda fill: jnp.full((Cn, HEAD_LANES), fill, F32)
        return (jnp.concatenate([c, ctx(1.0)]), jnp.concatenate([sa, ctx(0.0)]),
                jnp.concatenate([sb, ctx(0.0)]))

    return tables(RET_DK, 0) + tables(MLA_ROPE, MLA_NOPE)


def _stacked_weights(w_in, w_mla_qb, w_mla_kvb):
    depth = w_in.shape[0]
    split_at = np.cumsum(IN_SIZES)[:-1].tolist()
    rq, rk, rv, rg, cq, ckv, kr, gq, gk, gv, gates = (
        w.astype(BF16) for w in jnp.split(w_in, split_at, axis=-1))
    kr_pad = jnp.pad(kr, ((0, 0), (0, 0), (MLA_NOPE, HEAD_LANES - MLA_NOPE - MLA_ROPE)))
    qb = w_mla_qb.astype(BF16).reshape(depth, MLA_Q_RANK, MLA_HEADS, MLA_NOPE + MLA_ROPE)
    qb = jnp.pad(qb, ((0, 0), (0, 0), (0, 0), (0, HEAD_LANES - MLA_NOPE - MLA_ROPE)))
    kvb = w_mla_kvb.astype(BF16).reshape(depth, MLA_KV_RANK, MLA_HEADS, MLA_NOPE + MLA_V)
    kb = jnp.pad(kvb[..., :MLA_NOPE], ((0, 0), (0, 0), (0, 0), (0, HEAD_LANES - MLA_NOPE)))
    flat = lambda w: w.reshape(depth, w.shape[1], -1)
    wT = lambda w: jnp.swapaxes(w, 1, 2)
    natural = (rq, rk, rv, rg, ckv, kr_pad, gk, flat(kb))
    transposed = (wT(cq), wT(flat(qb)), wT(flat(kvb[..., MLA_NOPE:])), wT(gq), wT(gv))
    return natural, transposed, gates


def kernel(x, c, ctx, c_ctx, w_mod, b_mod, g_mix, w_in, g_mla_q, g_mla_kv, w_mla_qb, w_mla_kvb,
           g_gqa_q, g_gqa_k, w_ret_o, w_mla_o, w_gqa_o, w_out, g_ffn, w_ffn_in, w_ffn_out, g_final):
    B, L, D = x.shape
    Cn = ctx.shape[1]
    depth = w_mod.shape[0]
    assert B == 1 and D == D_MODEL and L % ROW_TILE == 0 and Cn % ROW_TILE == 0 and L % Cn == 0
    assert L % GRID_W == 0
    n_lat_tiles = L // ROW_TILE
    ffn_hidden = w_ffn_out.shape[1]

    xs = jnp.concatenate([x[0], ctx[0]], axis=0)
    cond = jnp.zeros((MOD_ROWS, D), F32).at[0].set(c[0]).at[1].set(c_ctx)
    mods = _mod_call(cond, w_mod, b_mod)
    mods = mods[:, :2].reshape(depth, 2, N_MOD, D)
    mods = jnp.pad(mods, ((0, 0), (0, 0), (0, MOD_ROWS - N_MOD), (0, 0)))

    tabs = _rope_tables(L, Cn)
    ret_consts = _ret_consts()
    rows3 = lambda g: g.reshape(depth, 1, -1)
    w_nat, w_tr, w_gates = _stacked_weights(w_in, w_mla_qb, w_mla_kvb)
    lanes = lambda g: jnp.broadcast_to(g[:, :, None], g.shape + (ROW_TILE,))
    wts = w_nat + w_tr + (rows3(g_mla_kv), rows3(g_gqa_k), lanes(g_mla_q), lanes(g_gqa_q))
    tabsT = tuple(t.T for t in tabs)
    merge_w = (w_gates,) + tuple(w.astype(BF16) for w in (w_ret_o, w_mla_o, w_gqa_o, w_out))
    w_ffn_a = w_ffn_in[..., :ffn_hidden].astype(BF16)
    w_ffn_b = w_ffn_in[..., ffn_hidden:].astype(BF16)
    w_ffn_o = w_ffn_out.astype(BF16)
    g_mix3, g_ffn3 = rows3(g_mix), rows3(g_ffn)

    for l in range(depth):
        rq, rk, rv, rg, mqT, mk, mvT, gqT, gk, gvT = _inproj_call(
            xs, mods, g_mix3, tabs, tabsT, wts, n_lat_tiles, l)
        yf, yb = _ret_call(rq, rk, rv, ret_consts, L // RET_CHUNK, Cn // RET_CHUNK)
        ym = _attention(mqT, mk, mvT, 1, L, Cn, "mla")
        yg = _attention(gqT, gk, gvT, GQA_HEADS // GQA_KV_HEADS, L, Cn, "gqa")
        last = l == depth - 1
        xs = _mix_ffn_call(xs, mods, g_mix3, yf, yb, rg, ym, yg, merge_w, g_ffn3,
                           (w_ffn_a, w_ffn_b, w_ffn_o), g_final.reshape(1, -1), n_lat_tiles, l,
                           rows=L if last else L + Cn, final_norm=last)
    return xs[None]
```

```python
import functools
import math

import jax
import jax.numpy as jnp
import numpy as np
from jax import lax
from jax.experimental import pallas as pl
from jax.experimental.pallas import tpu as pltpu

D_MODEL = 1024
GRID_W = 64
RET_CHUNK = 128
ROPE_BASE = 10000.0
NORM_EPS = 1e-6
N_MOD = 6

RET_HEADS = 4
RET_DK = 128
RET_DV = 256
RET_DECAY_START = 5.0
RET_BWD_OFFSET = 0.5

MLA_HEADS = 8
MLA_Q_RANK = 256
MLA_KV_RANK = 256
MLA_NOPE = 64
MLA_ROPE = 32
MLA_V = 128

GQA_HEADS = 8
GQA_KV_HEADS = 2
GQA_HD = 128

RET_W = RET_HEADS * RET_DV
MLA_W = MLA_HEADS * MLA_V
GQA_W = GQA_HEADS * GQA_HD
N_BRANCH = 3

IN_SIZES = (
    RET_HEADS * RET_DK, RET_HEADS * RET_DK, RET_W, RET_W,
    MLA_Q_RANK, MLA_KV_RANK, MLA_ROPE,
    GQA_W, GQA_KV_HEADS * GQA_HD, GQA_KV_HEADS * GQA_HD,
    N_BRANCH * D_MODEL,
)

HEAD_LANES = 128
BF16_SUBLANES = 16
V_ROWS = HEAD_LANES + BF16_SUBLANES
MOD_ROWS = 8
ROW_TILE = 256
FLASH_LAG = 2
FLASH_SUB = 256
V7X_VMEM_LIMIT = 56 * 1024 * 1024
LOG2E = math.log2(math.e)
NEG_BIG = -1e30

F32 = jnp.float32
BF16 = jnp.bfloat16


def _params(*sem):
    return pltpu.CompilerParams(dimension_semantics=sem, vmem_limit_bytes=V7X_VMEM_LIMIT)


def _const_spec(shape):
    zeros = (0,) * len(shape)
    return pl.BlockSpec(shape, lambda *_: zeros)


def _layer_spec(w, l, pipeline_mode=None):
    zeros = (0,) * (w.ndim - 1)
    return pl.BlockSpec((None,) + w.shape[1:], lambda *_: (l,) + zeros, pipeline_mode=pipeline_mode)


def _mod_spec(l, n_lat_tiles):
    return pl.BlockSpec((None, 1, MOD_ROWS, D_MODEL), lambda i: (l, i // n_lat_tiles, 0, 0))


def _stream_specs(stream, n_lat_tiles):
    _, _, ctx_tile0 = stream
    tile = (ROW_TILE, D_MODEL)
    return [pl.BlockSpec(tile, lambda i: (jnp.minimum(i, n_lat_tiles - 1), 0)),
            pl.BlockSpec(tile, lambda i: (ctx_tile0 + jnp.maximum(i - n_lat_tiles, 0), 0))]


def _stream_tile(lat_ref, ctx_ref, n_lat_tiles):
    return jnp.where(pl.program_id(0) < n_lat_tiles, lat_ref[...], ctx_ref[...])


def _dot(a, b):
    return jnp.dot(a, b, preferred_element_type=F32)


def _rms(x):
    return x * lax.rsqrt(jnp.mean(x * x, axis=-1, keepdims=True) + NORM_EPS)


def _silu(x):
    return x * jax.nn.sigmoid(x)


def _rope(x, c, sa, sb, block):
    up = pltpu.roll(x, HEAD_LANES - block, 1)
    dn = pltpu.roll(x, block, 1)
    return x * c + up * sa + dn * sb


def _head(x, h):
    return x[:, h * HEAD_LANES:(h + 1) * HEAD_LANES]


def _mod_kernel(cond_ref, w_ref, b_ref, o_ref):
    s = _silu(cond_ref[...]).astype(BF16)
    o_ref[0] = _dot(s, w_ref[0].astype(BF16)) + b_ref[0]


def _mod_call(cond, w_mod, b_mod):
    depth, d, n = w_mod.shape
    tn = n // 4
    return pl.pallas_call(
        _mod_kernel,
        out_shape=jax.ShapeDtypeStruct((depth, MOD_ROWS, n), F32),
        grid=(depth, n // tn),
        in_specs=[
            pl.BlockSpec((MOD_ROWS, d), lambda l, j: (0, 0)),
            pl.BlockSpec((1, d, tn), lambda l, j: (l, 0, j)),
            pl.BlockSpec((1, 1, tn), lambda l, j: (l, 0, j)),
        ],
        out_specs=pl.BlockSpec((1, MOD_ROWS, tn), lambda l, j: (l, 0, j)),
        compiler_params=_params("arbitrary", "arbitrary"),
        name="mod",
    )(cond, w_mod, b_mod.reshape(depth, 1, n))


def _prenorm(x, mod, g, shift_row, scale_row):
    h = _rms(x) * g
    return h * (1.0 + mod[scale_row:scale_row + 1]) + mod[shift_row:shift_row + 1]


def _rms_rows(x):
    return x * lax.rsqrt(jnp.mean(x * x, axis=0, keepdims=True) + NORM_EPS)


def _rope_rows(x, c, sa, sb, block):
    up = jnp.concatenate([x[block:], x[:block]], axis=0)
    dn = jnp.concatenate([x[-block:], x[:-block]], axis=0)
    return x * c + up * sa + dn * sb


def _inproj_kernel(x_ref, xc_ref, mod_ref, gmix_ref, c_ref, sa_ref, sb_ref, cm_ref, sam_ref, sbm_ref,
                   cT_ref, saT_ref, sbT_ref, cmT_ref, samT_ref, sbmT_ref,
                   w_rq, w_rk, w_rv, w_rg, w_ckv, w_kr, w_gk, w_kb,
                   w_cqT, w_qbT, w_vbT, w_gqT, w_gvT, g_mkv, g_gk, g_mqT, g_gqT,
                   rq_o, rk_o, rv_o, rg_o, mqT_o, mk_o, mvT_o, gqT_o, gk_o, gvT_o, *, n_lat_tiles):
    h = _prenorm(_stream_tile(x_ref, xc_ref, n_lat_tiles), mod_ref[0], gmix_ref[...], 0, 1)
    hb = h.astype(BF16)
    hTb = h.T.astype(BF16)
    c, sa, sb = c_ref[...], sa_ref[...], sb_ref[...]
    cm, sam, sbm = cm_ref[...], sam_ref[...], sbm_ref[...]
    rope_big = lambda v: _rope(v, c, sa, sb, RET_DK // 4)
    rope_mla = lambda v: _rope(v, cm, sam, sbm, MLA_ROPE // 4)
    rows = lambda v, i: v[i * HEAD_LANES:(i + 1) * HEAD_LANES]
    ones_rows = (lax.broadcasted_iota(jnp.int32, (BF16_SUBLANES, x_ref.shape[0]), 0) == 0).astype(BF16)

    cqnT = (_rms_rows(_dot(w_cqT[...], hTb)) * g_mqT[...]).astype(BF16)
    ckvn = _rms(_dot(hb, w_ckv[...])) * g_mkv[...]

    rq = _dot(hb, w_rq[...])
    rk = _dot(hb, w_rk[...])
    for i in range(RET_HEADS):
        sl = slice(i * HEAD_LANES, (i + 1) * HEAD_LANES)
        rq_o[:, sl] = rope_big(_head(rq, i))
        rk_o[:, sl] = rope_big(_head(rk, i)) * (RET_DK ** -0.5)

    gqa_qscale = GQA_HD ** -0.5 * LOG2E
    gqT = _dot(w_gqT[...], hTb)
    cT, saT, sbT = cT_ref[...], saT_ref[...], sbT_ref[...]
    for i in range(GQA_HEADS):
        qn = _rms_rows(rows(gqT, i)) * g_gqT[...]
        gqT_o[i] = (_rope_rows(qn, cT, saT, sbT, GQA_HD // 4) * gqa_qscale).astype(BF16)
    gk = _dot(hb, w_gk[...])
    gvT = _dot(w_gvT[...], hTb)
    for i in range(GQA_KV_HEADS):
        sl = slice(i * HEAD_LANES, (i + 1) * HEAD_LANES)
        gk_o[:, sl] = rope_big(_rms(_head(gk, i)) * g_gk[...]).astype(BF16)
        gvT_o[i, :HEAD_LANES, :] = rows(gvT, i).astype(BF16)
        gvT_o[i, HEAD_LANES:, :] = ones_rows

    rv_o[...] = _dot(hb, w_rv[...]).astype(BF16)
    rg_o[...] = _dot(hb, w_rg[...])

    mla_qscale = (MLA_NOPE + MLA_ROPE) ** -0.5 * LOG2E
    kr = rope_mla(_dot(hb, w_kr[...]))
    qT = _dot(w_qbT[...], cqnT)
    kn = _dot(ckvn.astype(BF16), w_kb[...])
    vT = _dot(w_vbT[...], ckvn.T.astype(BF16))
    cmT, samT, sbmT = cmT_ref[...], samT_ref[...], sbmT_ref[...]
    for i in range(MLA_HEADS):
        sl = slice(i * HEAD_LANES, (i + 1) * HEAD_LANES)
        mqT_o[i] = (_rope_rows(rows(qT, i), cmT, samT, sbmT, MLA_ROPE // 4) * mla_qscale).astype(BF16)
        mk_o[:, sl] = (_head(kn, i) + kr).astype(BF16)
        mvT_o[i, :HEAD_LANES, :] = rows(vT, i).astype(BF16)
        mvT_o[i, HEAD_LANES:, :] = ones_rows


def _inproj_call(stream, mods, gmix, tabs, tabsT, wts, n_lat_tiles, l):
    T, D = tabs[0].shape[0], D_MODEL
    tm = ROW_TILE
    row = lambda w: pl.BlockSpec((tm, w), lambda i: (i, 0))
    col = pl.BlockSpec((HEAD_LANES, tm), lambda i: (0, i))
    headsT = lambda nh, rows=HEAD_LANES: pl.BlockSpec((nh, rows, tm), lambda i: (0, 0, i))
    in_specs = _stream_specs(stream, n_lat_tiles) + [_mod_spec(l, n_lat_tiles), _layer_spec(gmix, l)]
    in_specs += [row(HEAD_LANES)] * 6 + [col] * 6 + [_layer_spec(w, l) for w in wts]
    out_shape = (
        jax.ShapeDtypeStruct((T, RET_HEADS * RET_DK), F32),
        jax.ShapeDtypeStruct((T, RET_HEADS * RET_DK), F32),
        jax.ShapeDtypeStruct((T, RET_W), BF16),
        jax.ShapeDtypeStruct((T, RET_W), F32),
        jax.ShapeDtypeStruct((MLA_HEADS, HEAD_LANES, T), BF16),
        jax.ShapeDtypeStruct((T, MLA_HEADS * HEAD_LANES), BF16),
        jax.ShapeDtypeStruct((MLA_HEADS, V_ROWS, T), BF16),
        jax.ShapeDtypeStruct((GQA_HEADS, HEAD_LANES, T), BF16),
        jax.ShapeDtypeStruct((T, GQA_KV_HEADS * GQA_HD), BF16),
        jax.ShapeDtypeStruct((GQA_KV_HEADS, V_ROWS, T), BF16),
    )
    out_specs = (
        row(RET_HEADS * RET_DK), row(RET_HEADS * RET_DK), row(RET_W), row(RET_W),
        headsT(MLA_HEADS), row(MLA_HEADS * HEAD_LANES), headsT(MLA_HEADS, V_ROWS),
        headsT(GQA_HEADS), row(GQA_KV_HEADS * GQA_HD), headsT(GQA_KV_HEADS, V_ROWS),
    )
    return pl.pallas_call(
        functools.partial(_inproj_kernel, n_lat_tiles=n_lat_tiles),
        out_shape=out_shape,
        grid=(T // tm,),
        in_specs=in_specs,
        out_specs=out_specs,
        compiler_params=_params("arbitrary"),
        name="inproj",
    )(stream[0], stream[1], mods, gmix, *tabs, *tabsT, *wts)


RET_CHUNKS_PER_STEP = 2


def _ret_kernel(qf_ref, kf_ref, vf_ref, qb_ref, kb_ref, vb_ref,
                dec_ref, xi_ref, zeta_ref, cd_ref,
                yf_ref, yb_ref, sf_ref, sb_ref):
    @pl.when(pl.program_id(0) == 0)
    def _():
        sf_ref[...] = jnp.zeros_like(sf_ref)
        sb_ref[...] = jnp.zeros_like(sb_ref)

    C = RET_CHUNK
    dirs = ((qf_ref, kf_ref, vf_ref, yf_ref, sf_ref, range(RET_CHUNKS_PER_STEP)),
            (qb_ref, kb_ref, vb_ref, yb_ref, sb_ref, range(RET_CHUNKS_PER_STEP - 1, -1, -1)))
    states = [[s_ref[h] for h in range(RET_HEADS)] for *_, s_ref, _ in dirs]
    pending = []
    for pos in range(RET_CHUNKS_PER_STEP):
        for d, (q_ref, k_ref, v_ref, y_ref, s_ref, order) in enumerate(dirs):
            rows = pl.ds(order[pos] * C, C)
            for h in range(RET_HEADS):
                ksl = slice(h * RET_DK, (h + 1) * RET_DK)
                vsl = slice(h * RET_DV, (h + 1) * RET_DV)
                k = k_ref[rows, ksl]
                v = v_ref[rows, vsl]
                qb16 = q_ref[rows, ksl].astype(BF16)
                scores = lax.dot_general(qb16, k.astype(BF16), (((1,), (1,)), ((), ())),
                                         preferred_element_type=F32) * dec_ref[d, h]
                state = states[d][h]
                cross = _dot(qb16, state.astype(BF16)) * xi_ref[d, h]
                kz = (k * zeta_ref[d, h]).T.astype(BF16)
                states[d][h] = cd_ref[d, h] * state + _dot(kz, v)
                pending.append((y_ref, rows, vsl, scores.astype(BF16), v, cross))
    for d, (*_, s_ref, _) in enumerate(dirs):
        for h in range(RET_HEADS):
            s_ref[h] = states[d][h]
    for y_ref, rows, vsl, scores, v, cross in pending:
        y_ref[rows, vsl] = _dot(scores, v) + cross


def _ret_call(rq, rk, rv, consts, n_lat, n_ctx):
    T = rq.shape[0]
    rows = RET_CHUNK * RET_CHUNKS_PER_STEP
    assert n_lat % RET_CHUNKS_PER_STEP == 0 and n_ctx % RET_CHUNKS_PER_STEP == 0
    g_lat, g_ctx = n_lat // RET_CHUNKS_PER_STEP, n_ctx // RET_CHUNKS_PER_STEP
    n = g_lat + g_ctx

    def fwd(i):
        return jnp.where(i < g_ctx, g_lat + i, i - g_ctx)

    def bwd(i):
        return n - 1 - i

    kw, vw = RET_HEADS * RET_DK, RET_W
    spec = lambda w, order: pl.BlockSpec((rows, w), lambda i: (order(i), 0))
    dec, xi, zeta, cd = consts
    return pl.pallas_call(
        _ret_kernel,
        out_shape=(jax.ShapeDtypeStruct((T, vw), F32), jax.ShapeDtypeStruct((T, vw), F32)),
        grid=(n,),
        in_specs=[spec(kw, fwd), spec(kw, fwd), spec(vw, fwd),
                  spec(kw, bwd), spec(kw, bwd), spec(vw, bwd),
                  _const_spec(dec.shape), _const_spec(xi.shape),
                  _const_spec(zeta.shape), _const_spec(cd.shape)],
        out_specs=(spec(vw, fwd), spec(vw, bwd)),
        scratch_shapes=[pltpu.VMEM((RET_HEADS, RET_DK, RET_DV), F32),
                        pltpu.VMEM((RET_HEADS, RET_DK, RET_DV), F32)],
        compiler_params=_params("arbitrary"),
        name="retention",
    )(rq, rk, rv, rq, rk, rv, dec, xi, zeta, cd)


def _ret_consts():
    C = RET_CHUNK
    h = jnp.arange(RET_HEADS, dtype=F32)
    pos = jnp.arange(C, dtype=F32)
    diff = pos[:, None] - pos[None, :]

    def one(offset, reverse):
        lg = jnp.log1p(-jnp.exp2(-(RET_DECAY_START + offset) - h))
        dd = -diff if reverse else diff
        keep = dd >= 0
        dec = jnp.where(keep[None], jnp.exp(lg[:, None, None] * jnp.where(keep, dd, 0.0)[None]), 0.0)
        p = (C - 1 - pos) if reverse else pos
        xi = jnp.exp(lg[:, None] * (p + 1)[None, :])
        zeta = jnp.exp(lg[:, None] * (C - 1 - p)[None, :])
        cd = jnp.exp(lg * C)
        return dec, xi[:, :, None], zeta[:, :, None], jnp.broadcast_to(cd[:, None, None], (RET_HEADS, 1, 1))

    f = one(0.0, False)
    b = one(RET_BWD_OFFSET, True)
    return tuple(jnp.stack([a, c]) for a, c in zip(f, b))


def _chunk_colors(nk):
    n_main = (nk // 3) * 3
    if n_main == 0:
        return tuple(range(nk))
    return tuple(c % 3 if c < n_main else 3 + c - n_main for c in range(nk))


def _flash_kernel(qT_ref, k_ref, vT_ref, o_ref, m_ref, acc_ref, *buf_refs, tq, nq, tk, nk, sub, n_ctx):
    lag = FLASH_LAG
    colors = _chunk_colors(nk)
    n_buf = max(colors) + 1
    s_bufs, c_bufs = buf_refs[:n_buf], buf_refs[n_buf:]
    n_sub = tk // sub
    n_loop = max(((nk // 3) * 3 - lag) // 3, 0)

    def key_rows(c, r):
        base = c * tk if isinstance(c, int) else pl.multiple_of(c * tk, tk)
        return pl.ds(base + r * sub, sub)

    def q_cols(i):
        return pl.ds(i * tq if isinstance(i, int) else pl.multiple_of(i * tq, tq), tq)

    def tick(consume, score):
        if consume is not None:
            c, buf = consume
            m_prev = m_ref[...]
            m_new = jnp.maximum(m_prev, c_bufs[buf][...])
            alpha = jnp.exp2(m_prev - m_new)
            m_ref[...] = m_new
        cmax = pv = None
        for r in range(n_sub):
            rows = pl.ds(r * sub, sub)
            if consume is not None:
                p = jnp.exp2(s_bufs[buf][rows, :] - m_new).astype(BF16)
                part = _dot(vT_ref[0, :, key_rows(c, r)], p)
                pv = part if pv is None else pv + part
            if score is not None:
                s = _dot(k_ref[key_rows(score[0], r), :], score[2])
                s_bufs[score[1]][rows, :] = s
                part = jnp.max(s, axis=0, keepdims=True)
                cmax = part if cmax is None else jnp.maximum(cmax, part)
        if score is not None:
            c_bufs[score[1]][...] = cmax
        if consume is not None:
            acc_ref[...] = alpha * acc_ref[...] + pv

    def q_tile(i, carry, has_next=True):
        q = qT_ref[0, :, q_cols(i)]
        q_next = qT_ref[0, :, q_cols(i + 1)] if has_next else None
        m_ref[...] = jnp.full_like(m_ref, NEG_BIG)
        acc_ref[...] = jnp.zeros_like(acc_ref)

        def body(t, carry):
            for u in range(3):
                tick((t * 3 + u, u), (t * 3 + u + lag, (u + lag) % 3, q))
            return carry

        if n_loop:
            lax.fori_loop(0, n_loop, body, 0)
        for c in range(3 * n_loop, nk):
            ahead = c + lag
            if ahead < nk:
                score = (ahead, colors[ahead], q)
            elif has_next and ahead - nk < min(lag, nk):
                score = (ahead - nk, colors[ahead - nk], q_next)
            else:
                score = None
            tick((c, colors[c]), score)
        acc = acc_ref[...]
        o_ref[q_cols(i), :] = (acc[:HEAD_LANES] / acc[HEAD_LANES:HEAD_LANES + 1]).T.astype(o_ref.dtype)
        return carry

    q0 = qT_ref[0, :, q_cols(0)]
    for c in range(min(lag, nk)):
        tick(None, (c, colors[c], q0))
    if nq > 1:
        lax.fori_loop(0, nq - 1, q_tile, 0)
    q_tile(nq - 1, 0, has_next=False)

    ctx = pl.ds(nq * tq, n_ctx)
    s = _dot(k_ref[ctx, :], qT_ref[0, :, ctx])
    p = jnp.exp2(s - jnp.max(s, axis=0, keepdims=True)).astype(BF16)
    pv = _dot(vT_ref[0, :, ctx], p)
    o_ref[ctx, :] = (pv[:HEAD_LANES] / pv[HEAD_LANES:HEAD_LANES + 1]).T.astype(o_ref.dtype)


def _pick_tile(n, candidates):
    for t in candidates:
        if n % t == 0:
            return t
    raise ValueError(f"no tile for {n}")


def _attention(qT, k, vT, group, L, Cn, name):
    hq, T = qT.shape[0], qT.shape[2]
    tq = _pick_tile(L, (1024, 512, 256, 128))
    tk = _pick_tile(T, (1280, 256, 128))
    sub = min(tk, FLASH_SUB)
    assert T == L + Cn and tk % sub == 0 and Cn % HEAD_LANES == 0
    nq, nk = L // tq, T // tk
    n_buf = max(_chunk_colors(nk)) + 1
    return pl.pallas_call(
        functools.partial(_flash_kernel, tq=tq, nq=nq, tk=tk, nk=nk, sub=sub, n_ctx=Cn),
        out_shape=jax.ShapeDtypeStruct((T, hq * HEAD_LANES), BF16),
        grid=(hq,),
        in_specs=[
            pl.BlockSpec((1, HEAD_LANES, T), lambda h: (h, 0, 0), pipeline_mode=pl.Buffered(1)),
            pl.BlockSpec((T, HEAD_LANES), lambda h: (0, h // group)),
            pl.BlockSpec((1, V_ROWS, T), lambda h: (h // group, 0, 0)),
        ],
        out_specs=pl.BlockSpec((T, HEAD_LANES), lambda h: (0, h)),
        scratch_shapes=[pltpu.VMEM((1, tq), F32), pltpu.VMEM((V_ROWS, tq), F32)]
                       + [pltpu.VMEM((tk, tq), F32)] * n_buf + [pltpu.VMEM((1, tq), F32)] * n_buf,
        compiler_params=_params("arbitrary"),
        name=name,
    )(qT, k, vT)


def _mix_ffn_kernel(x_ref, xc_ref, mod_ref, gmix_ref, yf_ref, yb_ref, rg_ref, ym_ref, yg_ref,
                    w_gates, w_ro, w_mo, w_go, w_out, gffn_ref, w_a, w_b, w_o, gf_ref, o_ref,
                    *, final_norm, n_lat_tiles):
    x = _stream_tile(x_ref, xc_ref, n_lat_tiles)
    mod = mod_ref[0]
    hb = _prenorm(x, mod, gmix_ref[...], 0, 1).astype(BF16)
    gs = jax.nn.sigmoid(_dot(hb, w_gates[...]))
    y = yf_ref[...] + yb_ref[...]
    g = _silu(rg_ref[...])
    parts = []
    for h in range(RET_HEADS):
        sl = slice(h * RET_DV, (h + 1) * RET_DV)
        parts.append((g[:, sl] * _rms(y[:, sl])).astype(BF16))
    y_ret = jnp.concatenate(parts, axis=-1)
    D = D_MODEL
    z = (gs[:, :D] * _dot(y_ret, w_ro[...])
         + gs[:, D:2 * D] * _dot(ym_ref[...], w_mo[...])
         + gs[:, 2 * D:] * _dot(yg_ref[...], w_go[...]))
    x = x + mod[2:3] * _dot(z.astype(BF16), w_out[...])

    hb = _prenorm(x, mod, gffn_ref[...], 3, 4).astype(BF16)
    u = (_silu(_dot(hb, w_a[...])) * _dot(hb, w_b[...])).astype(BF16)
    x = x + mod[5:6] * _dot(u, w_o[...])
    o_ref[...] = _rms(x) * gf_ref[...] if final_norm else x


def _mix_ffn_call(stream, mods, gmix, yf, yb, rg, ym, yg, merge_w, g_ffn, ffn_w, g_final, n_lat_tiles, l,
                  rows, final_norm):
    D = D_MODEL
    tm = ROW_TILE
    row = lambda w: pl.BlockSpec((tm, w), lambda i: (i, 0))
    once = lambda w: _layer_spec(w, l, pl.Buffered(1))
    return pl.pallas_call(
        functools.partial(_mix_ffn_kernel, final_norm=final_norm, n_lat_tiles=n_lat_tiles),
        out_shape=jax.ShapeDtypeStruct((rows, D), F32),
        grid=(rows // tm,),
        in_specs=_stream_specs(stream, n_lat_tiles) + [_mod_spec(l, n_lat_tiles), _layer_spec(gmix, l),
                  row(RET_W), row(RET_W), row(RET_W), row(MLA_W), row(GQA_W)]
                 + [once(w) for w in merge_w] + [_layer_spec(g_ffn, l)] + [once(w) for w in ffn_w]
                 + [_const_spec((1, D))],
        out_specs=row(D),
        compiler_params=_params("arbitrary"),
        name="mix_ffn",
    )(stream[0], stream[1], mods, gmix, yf, yb, rg, ym, yg, *merge_w, g_ffn, *ffn_w, g_final)


def _rope_tables(L, Cn):
    n_rows = L // GRID_W

    def angles(n, dim):
        half = dim // 2
        inv_freq = ROPE_BASE ** (-jnp.arange(half, dtype=F32) / half)
        ang = jnp.arange(n, dtype=jnp.int32).astype(F32)[:, None] * inv_freq[None, :]
        return jnp.cos(ang), jnp.sin(ang)

    def tables(dim, lane0):
        cr, sr = (jnp.repeat(a, GRID_W, axis=0) for a in angles(n_rows, dim // 2))
        cc, sc = (jnp.tile(a, (n_rows, 1)) for a in angles(GRID_W, dim // 2))
        z = jnp.zeros_like(sr)
        c = jnp.concatenate([cr, cr, cc, cc], axis=1)
        sa = jnp.concatenate([-sr, z, -sc, z], axis=1)
        sb = jnp.concatenate([z, sr, z, sc], axis=1)
        pad = lambda a, fill: jnp.concatenate(
            [jnp.full((L, lane0), fill, F32), a, jnp.full((L, HEAD_LANES - lane0 - dim), fill, F32)], axis=1)
        c, sa, sb = pad(c, 1.0), pad(sa, 0.0), pad(sb, 0.0)
        ctx = lambda fill: jnp.full((Cn, HEAD_LANES), fill, F32)
        return (jnp.concatenate([c, ctx(1.0)]), jnp.concatenate([sa, ctx(0.0)]),
                jnp.concatenate([sb, ctx(0.0)]))

    return tables(RET_DK, 0) + tables(MLA_ROPE, MLA_NOPE)


def _stacked_weights(w_in, w_mla_qb, w_mla_kvb):
    depth = w_in.shape[0]
    split_at = np.cumsum(IN_SIZES)[:-1].tolist()
    rq, rk, rv, rg, cq, ckv, kr, gq, gk, gv, gates = (
        w.astype(BF16) for w in jnp.split(w_in, split_at, axis=-1))
    kr_pad = jnp.pad(kr, ((0, 0), (0, 0), (MLA_NOPE, HEAD_LANES - MLA_NOPE - MLA_ROPE)))
    qb = w_mla_qb.astype(BF16).reshape(depth, MLA_Q_RANK, MLA_HEADS, MLA_NOPE + MLA_ROPE)
    qb = jnp.pad(qb, ((0, 0), (0, 0), (0, 0), (0, HEAD_LANES - MLA_NOPE - MLA_ROPE)))
    kvb = w_mla_kvb.astype(BF16).reshape(depth, MLA_KV_RANK, MLA_HEADS, MLA_NOPE + MLA_V)
    kb = jnp.pad(kvb[..., :MLA_NOPE], ((0, 0), (0, 0), (0, 0), (0, HEAD_LANES - MLA_NOPE)))
    flat = lambda w: w.reshape(depth, w.shape[1], -1)
    wT = lambda w: jnp.swapaxes(w, 1, 2)
    natural = (rq, rk, rv, rg, ckv, kr_pad, gk, flat(kb))
    transposed = (wT(cq), wT(flat(qb)), wT(flat(kvb[..., MLA_NOPE:])), wT(gq), wT(gv))
    return natural, transposed, gates


def kernel(x, c, ctx, c_ctx, w_mod, b_mod, g_mix, w_in, g_mla_q, g_mla_kv, w_mla_qb, w_mla_kvb,
           g_gqa_q, g_gqa_k, w_ret_o, w_mla_o, w_gqa_o, w_out, g_ffn, w_ffn_in, w_ffn_out, g_final):
    B, L, D = x.shape
    Cn = ctx.shape[1]
    depth = w_mod.shape[0]
    assert B == 1 and D == D_MODEL and L % ROW_TILE == 0 and Cn % ROW_TILE == 0 and L % Cn == 0
    assert L % GRID_W == 0
    n_lat_tiles = L // ROW_TILE
    ffn_hidden = w_ffn_out.shape[1]

    stream = (x[0], ctx[0], 0)
    cond = jnp.zeros((MOD_ROWS, D), F32).at[0].set(c[0]).at[1].set(c_ctx)
    mods = _mod_call(cond, w_mod, b_mod)
    mods = mods[:, :2].reshape(depth, 2, N_MOD, D)
    mods = jnp.pad(mods, ((0, 0), (0, 0), (0, MOD_ROWS - N_MOD), (0, 0)))

    tabs = _rope_tables(L, Cn)
    ret_consts = _ret_consts()
    rows3 = lambda g: g.reshape(depth, 1, -1)
    w_nat, w_tr, w_gates = _stacked_weights(w_in, w_mla_qb, w_mla_kvb)
    lanes = lambda g: jnp.broadcast_to(g[:, :, None], g.shape + (ROW_TILE,))
    wts = w_nat + w_tr + (rows3(g_mla_kv), rows3(g_gqa_k), lanes(g_mla_q), lanes(g_gqa_q))
    tabsT = tuple(t.T for t in tabs)
    merge_w = (w_gates,) + tuple(w.astype(BF16) for w in (w_ret_o, w_mla_o, w_gqa_o, w_out))
    w_ffn_a = w_ffn_in[..., :ffn_hidden].astype(BF16)
    w_ffn_b = w_ffn_in[..., ffn_hidden:].astype(BF16)
    w_ffn_o = w_ffn_out.astype(BF16)
    g_mix3, g_ffn3 = rows3(g_mix), rows3(g_ffn)

    for l in range(depth):
        rq, rk, rv, rg, mqT, mk, mvT, gqT, gk, gvT = _inproj_call(
            stream, mods, g_mix3, tabs, tabsT, wts, n_lat_tiles, l)
        yf, yb = _ret_call(rq, rk, rv, ret_consts, L // RET_CHUNK, Cn // RET_CHUNK)
        ym = _attention(mqT, mk, mvT, 1, L, Cn, "mla")
        yg = _attention(gqT, gk, gvT, GQA_HEADS // GQA_KV_HEADS, L, Cn, "gqa")
        last = l == depth - 1
        xs = _mix_ffn_call(stream, mods, g_mix3, yf, yb, rg, ym, yg, merge_w, g_ffn3,
                           (w_ffn_a, w_ffn_b, w_ffn_o), g_final.reshape(1, -1), n_lat_tiles, l,
                           rows=L if last else L + Cn, final_norm=last)
        stream = (xs, xs, n_lat_tiles)
    return xs[None]
```

```python
import functools
import math

import jax
import jax.numpy as jnp
import numpy as np
from jax import lax
from jax.experimental import pallas as pl
from jax.experimental.pallas import tpu as pltpu

D_MODEL = 1024
GRID_W = 64
RET_CHUNK = 128
ROPE_BASE = 10000.0
NORM_EPS = 1e-6
N_MOD = 6

RET_HEADS = 4
RET_DK = 128
RET_DV = 256
RET_DECAY_START = 5.0
RET_BWD_OFFSET = 0.5

MLA_HEADS = 8
MLA_Q_RANK = 256
MLA_KV_RANK = 256
MLA_NOPE = 64
MLA_ROPE = 32
MLA_V = 128

GQA_HEADS = 8
GQA_KV_HEADS = 2
GQA_HD = 128

RET_W = RET_HEADS * RET_DV
MLA_W = MLA_HEADS * MLA_V
GQA_W = GQA_HEADS * GQA_HD
N_BRANCH = 3

IN_SIZES = (
    RET_HEADS * RET_DK, RET_HEADS * RET_DK, RET_W, RET_W,
    MLA_Q_RANK, MLA_KV_RANK, MLA_ROPE,
    GQA_W, GQA_KV_HEADS * GQA_HD, GQA_KV_HEADS * GQA_HD,
    N_BRANCH * D_MODEL,
)

HEAD_LANES = 128
BF16_SUBLANES = 16
V_ROWS = HEAD_LANES + BF16_SUBLANES
MOD_ROWS = 8
ROW_TILE = 256
FFN_SLAB_ALIGN = 256
FLASH_LAG = 2
FLASH_SUB = 256
V7X_VMEM_LIMIT = 56 * 1024 * 1024
LOG2E = math.log2(math.e)
NEG_BIG = -1e30

F32 = jnp.float32
BF16 = jnp.bfloat16


def _params(*sem):
    return pltpu.CompilerParams(dimension_semantics=sem, vmem_limit_bytes=V7X_VMEM_LIMIT)


def _const_spec(shape):
    zeros = (0,) * len(shape)
    return pl.BlockSpec(shape, lambda *_: zeros)


def _layer_spec(w, l, pipeline_mode=None):
    zeros = (0,) * (w.ndim - 1)
    return pl.BlockSpec((None,) + w.shape[1:], lambda *_: (l,) + zeros, pipeline_mode=pipeline_mode)


def _mod_spec(l, n_lat_tiles):
    return pl.BlockSpec((None, 1, MOD_ROWS, D_MODEL), lambda i: (l, i // n_lat_tiles, 0, 0))


def _stream_specs(stream, n_lat_tiles):
    _, _, ctx_tile0 = stream
    tile = (ROW_TILE, D_MODEL)
    return [pl.BlockSpec(tile, lambda i: (jnp.minimum(i, n_lat_tiles - 1), 0)),
            pl.BlockSpec(tile, lambda i: (ctx_tile0 + jnp.maximum(i - n_lat_tiles, 0), 0))]


def _stream_tile(lat_ref, ctx_ref, n_lat_tiles):
    return jnp.where(pl.program_id(0) < n_lat_tiles, lat_ref[...], ctx_ref[...])


def _dot(a, b):
    return jnp.dot(a, b, preferred_element_type=F32)


def _rms(x):
    return x * lax.rsqrt(jnp.mean(x * x, axis=-1, keepdims=True) + NORM_EPS)


def _silu(x):
    return x * jax.nn.sigmoid(x)


def _rope(x, c, sa, sb, block):
    up = pltpu.roll(x, HEAD_LANES - block, 1)
    dn = pltpu.roll(x, block, 1)
    return x * c + up * sa + dn * sb


def _head(x, h):
    return x[:, h * HEAD_LANES:(h + 1) * HEAD_LANES]


def _mod_kernel(cond_ref, w_ref, b_ref, o_ref):
    s = _silu(cond_ref[...]).astype(BF16)
    o_ref[0] = _dot(s, w_ref[0].astype(BF16)) + b_ref[0]


def _mod_call(cond, w_mod, b_mod):
    depth, d, n = w_mod.shape
    tn = n // 4
    return pl.pallas_call(
        _mod_kernel,
        out_shape=jax.ShapeDtypeStruct((depth, MOD_ROWS, n), F32),
        grid=(depth, n // tn),
        in_specs=[
            pl.BlockSpec((MOD_ROWS, d), lambda l, j: (0, 0)),
            pl.BlockSpec((1, d, tn), lambda l, j: (l, 0, j)),
            pl.BlockSpec((1, 1, tn), lambda l, j: (l, 0, j)),
        ],
        out_specs=pl.BlockSpec((1, MOD_ROWS, tn), lambda l, j: (l, 0, j)),
        compiler_params=_params("arbitrary", "arbitrary"),
        name="mod",
    )(cond, w_mod, b_mod.reshape(depth, 1, n))


def _prenorm(x, mod, g, shift_row, scale_row):
    h = _rms(x) * g
    return h * (1.0 + mod[scale_row:scale_row + 1]) + mod[shift_row:shift_row + 1]


def _rms_rows(x):
    return x * lax.rsqrt(jnp.mean(x * x, axis=0, keepdims=True) + NORM_EPS)


def _rope_rows(x, c, sa, sb, block):
    up = jnp.concatenate([x[block:], x[:block]], axis=0)
    dn = jnp.concatenate([x[-block:], x[:-block]], axis=0)
    return x * c + up * sa + dn * sb


def _inproj_kernel(x_ref, xc_ref, mod_ref, gmix_ref, c_ref, sa_ref, sb_ref, cm_ref, sam_ref, sbm_ref,
                   cT_ref, saT_ref, sbT_ref, cmT_ref, samT_ref, sbmT_ref,
                   w_rq, w_rk, w_rv, w_rg, w_ckv, w_kr, w_gk, w_kb,
                   w_cqT, w_qbT, w_vbT, w_gqT, w_gvT, g_mkv, g_gk, g_mqT, g_gqT,
                   rq_o, rk_o, rv_o, rg_o, mqT_o, mk_o, mvT_o, gqT_o, gk_o, gvT_o, *, n_lat_tiles):
    h = _prenorm(_stream_tile(x_ref, xc_ref, n_lat_tiles), mod_ref[0], gmix_ref[...], 0, 1)
    hb = h.astype(BF16)
    hTb = h.T.astype(BF16)
    c, sa, sb = c_ref[...], sa_ref[...], sb_ref[...]
    cm, sam, sbm = cm_ref[...], sam_ref[...], sbm_ref[...]
    rope_big = lambda v: _rope(v, c, sa, sb, RET_DK // 4)
    rope_mla = lambda v: _rope(v, cm, sam, sbm, MLA_ROPE // 4)
    rows = lambda v, i: v[i * HEAD_LANES:(i + 1) * HEAD_LANES]
    ones_rows = (lax.broadcasted_iota(jnp.int32, (BF16_SUBLANES, x_ref.shape[0]), 0) == 0).astype(BF16)

    cqnT = (_rms_rows(_dot(w_cqT[...], hTb)) * g_mqT[...]).astype(BF16)
    ckvn = _rms(_dot(hb, w_ckv[...])) * g_mkv[...]

    rq = _dot(hb, w_rq[...])
    rk = _dot(hb, w_rk[...])
    for i in range(RET_HEADS):
        sl = slice(i * HEAD_LANES, (i + 1) * HEAD_LANES)
        rq_o[:, sl] = rope_big(_head(rq, i))
        rk_o[:, sl] = rope_big(_head(rk, i)) * (RET_DK ** -0.5)

    gqa_qscale = GQA_HD ** -0.5 * LOG2E
    gqT = _dot(w_gqT[...], hTb)
    cT, saT, sbT = cT_ref[...], saT_ref[...], sbT_ref[...]
    for i in range(GQA_HEADS):
        qn = _rms_rows(rows(gqT, i)) * g_gqT[...]
        gqT_o[i] = (_rope_rows(qn, cT, saT, sbT, GQA_HD // 4) * gqa_qscale).astype(BF16)
    gk = _dot(hb, w_gk[...])
    gvT = _dot(w_gvT[...], hTb)
    for i in range(GQA_KV_HEADS):
        sl = slice(i * HEAD_LANES, (i + 1) * HEAD_LANES)
        gk_o[:, sl] = rope_big(_rms(_head(gk, i)) * g_gk[...]).astype(BF16)
        gvT_o[i, :HEAD_LANES, :] = rows(gvT, i).astype(BF16)
        gvT_o[i, HEAD_LANES:, :] = ones_rows

    rv_o[...] = _dot(hb, w_rv[...]).astype(BF16)
    rg_o[...] = _dot(hb, w_rg[...])

    mla_qscale = (MLA_NOPE + MLA_ROPE) ** -0.5 * LOG2E
    kr = rope_mla(_dot(hb, w_kr[...]))
    qT = _dot(w_qbT[...], cqnT)
    kn = _dot(ckvn.astype(BF16), w_kb[...])
    vT = _dot(w_vbT[...], ckvn.T.astype(BF16))
    cmT, samT, sbmT = cmT_ref[...], samT_ref[...], sbmT_ref[...]
    for i in range(MLA_HEADS):
        sl = slice(i * HEAD_LANES, (i + 1) * HEAD_LANES)
        mqT_o[i] = (_rope_rows(rows(qT, i), cmT, samT, sbmT, MLA_ROPE // 4) * mla_qscale).astype(BF16)
        mk_o[:, sl] = (_head(kn, i) + kr).astype(BF16)
        mvT_o[i, :HEAD_LANES, :] = rows(vT, i).astype(BF16)
        mvT_o[i, HEAD_LANES:, :] = ones_rows


def _inproj_call(stream, mods, gmix, tabs, tabsT, wts, n_lat_tiles, l):
    T, D = tabs[0].shape[0], D_MODEL
    tm = ROW_TILE
    row = lambda w: pl.BlockSpec((tm, w), lambda i: (i, 0))
    col = pl.BlockSpec((HEAD_LANES, tm), lambda i: (0, i))
    headsT = lambda nh, rows=HEAD_LANES: pl.BlockSpec((nh, rows, tm), lambda i: (0, 0, i))
    in_specs = _stream_specs(stream, n_lat_tiles) + [_mod_spec(l, n_lat_tiles), _layer_spec(gmix, l)]
    in_specs += [row(HEAD_LANES)] * 6 + [col] * 6 + [_layer_spec(w, l) for w in wts]
    out_shape = (
        jax.ShapeDtypeStruct((T, RET_HEADS * RET_DK), F32),
        jax.ShapeDtypeStruct((T, RET_HEADS * RET_DK), F32),
        jax.ShapeDtypeStruct((T, RET_W), BF16),
        jax.ShapeDtypeStruct((T, RET_W), F32),
        jax.ShapeDtypeStruct((MLA_HEADS, HEAD_LANES, T), BF16),
        jax.ShapeDtypeStruct((T, MLA_HEADS * HEAD_LANES), BF16),
        jax.ShapeDtypeStruct((MLA_HEADS, V_ROWS, T), BF16),
        jax.ShapeDtypeStruct((GQA_HEADS, HEAD_LANES, T), BF16),
        jax.ShapeDtypeStruct((T, GQA_KV_HEADS * GQA_HD), BF16),
        jax.ShapeDtypeStruct((GQA_KV_HEADS, V_ROWS, T), BF16),
    )
    out_specs = (
        row(RET_HEADS * RET_DK), row(RET_HEADS * RET_DK), row(RET_W), row(RET_W),
        headsT(MLA_HEADS), row(MLA_HEADS * HEAD_LANES), headsT(MLA_HEADS, V_ROWS),
        headsT(GQA_HEADS), row(GQA_KV_HEADS * GQA_HD), headsT(GQA_KV_HEADS, V_ROWS),
    )
    return pl.pallas_call(
        functools.partial(_inproj_kernel, n_lat_tiles=n_lat_tiles),
        out_shape=out_shape,
        grid=(T // tm,),
        in_specs=in_specs,
        out_specs=out_specs,
        compiler_params=_params("arbitrary"),
        name="inproj",
    )(stream[0], stream[1], mods, gmix, *tabs, *tabsT, *wts)


RET_CHUNKS_PER_STEP = 2


def _ret_kernel(qf_ref, kf_ref, vf_ref, qb_ref, kb_ref, vb_ref,
                dec_ref, xi_ref, zeta_ref, cd_ref,
                yf_ref, yb_ref, sf_ref, sb_ref):
    @pl.when(pl.program_id(0) == 0)
    def _():
        sf_ref[...] = jnp.zeros_like(sf_ref)
        sb_ref[...] = jnp.zeros_like(sb_ref)

    C = RET_CHUNK
    dirs = ((qf_ref, kf_ref, vf_ref, yf_ref, sf_ref, range(RET_CHUNKS_PER_STEP)),
            (qb_ref, kb_ref, vb_ref, yb_ref, sb_ref, range(RET_CHUNKS_PER_STEP - 1, -1, -1)))
    states = [[s_ref[h] for h in range(RET_HEADS)] for *_, s_ref, _ in dirs]
    pending = []
    for pos in range(RET_CHUNKS_PER_STEP):
        for d, (q_ref, k_ref, v_ref, y_ref, s_ref, order) in enumerate(dirs):
            rows = pl.ds(order[pos] * C, C)
            for h in range(RET_HEADS):
                ksl = slice(h * RET_DK, (h + 1) * RET_DK)
                vsl = slice(h * RET_DV, (h + 1) * RET_DV)
                k = k_ref[rows, ksl]
                v = v_ref[rows, vsl]
                qb16 = q_ref[rows, ksl].astype(BF16)
                scores = lax.dot_general(qb16, k.astype(BF16), (((1,), (1,)), ((), ())),
                                         preferred_element_type=F32) * dec_ref[d, h]
                state = states[d][h]
                cross = _dot(qb16, state.astype(BF16)) * xi_ref[d, h]
                kz = (k * zeta_ref[d, h]).T.astype(BF16)
                states[d][h] = cd_ref[d, h] * state + _dot(kz, v)
                pending.append((y_ref, rows, vsl, scores.astype(BF16), v, cross))
    for d, (*_, s_ref, _) in enumerate(dirs):
        for h in range(RET_HEADS):
            s_ref[h] = states[d][h]
    for y_ref, rows, vsl, scores, v, cross in pending:
        y_ref[rows, vsl] = _dot(scores, v) + cross


def _ret_call(rq, rk, rv, consts, n_lat, n_ctx):
    T = rq.shape[0]
    rows = RET_CHUNK * RET_CHUNKS_PER_STEP
    assert n_lat % RET_CHUNKS_PER_STEP == 0 and n_ctx % RET_CHUNKS_PER_STEP == 0
    g_lat, g_ctx = n_lat // RET_CHUNKS_PER_STEP, n_ctx // RET_CHUNKS_PER_STEP
    n = g_lat + g_ctx

    def fwd(i):
        return jnp.where(i < g_ctx, g_lat + i, i - g_ctx)

    def bwd(i):
        return n - 1 - i

    kw, vw = RET_HEADS * RET_DK, RET_W
    spec = lambda w, order: pl.BlockSpec((rows, w), lambda i: (order(i), 0))
    dec, xi, zeta, cd = consts
    return pl.pallas_call(
        _ret_kernel,
        out_shape=(jax.ShapeDtypeStruct((T, vw), F32), jax.ShapeDtypeStruct((T, vw), F32)),
        grid=(n,),
        in_specs=[spec(kw, fwd), spec(kw, fwd), spec(vw, fwd),
                  spec(kw, bwd), spec(kw, bwd), spec(vw, bwd),
                  _const_spec(dec.shape), _const_spec(xi.shape),
                  _const_spec(zeta.shape), _const_spec(cd.shape)],
        out_specs=(spec(vw, fwd), spec(vw, bwd)),
        scratch_shapes=[pltpu.VMEM((RET_HEADS, RET_DK, RET_DV), F32),
                        pltpu.VMEM((RET_HEADS, RET_DK, RET_DV), F32)],
        compiler_params=_params("arbitrary"),
        name="retention",
    )(rq, rk, rv, rq, rk, rv, dec, xi, zeta, cd)


def _ret_consts():
    C = RET_CHUNK
    h = jnp.arange(RET_HEADS, dtype=F32)
    pos = jnp.arange(C, dtype=F32)
    diff = pos[:, None] - pos[None, :]

    def one(offset, reverse):
        lg = jnp.log1p(-jnp.exp2(-(RET_DECAY_START + offset) - h))
        dd = -diff if reverse else diff
        keep = dd >= 0
        dec = jnp.where(keep[None], jnp.exp(lg[:, None, None] * jnp.where(keep, dd, 0.0)[None]), 0.0)
        p = (C - 1 - pos) if reverse else pos
        xi = jnp.exp(lg[:, None] * (p + 1)[None, :])
        zeta = jnp.exp(lg[:, None] * (C - 1 - p)[None, :])
        cd = jnp.exp(lg * C)
        return dec, xi[:, :, None], zeta[:, :, None], jnp.broadcast_to(cd[:, None, None], (RET_HEADS, 1, 1))

    f = one(0.0, False)
    b = one(RET_BWD_OFFSET, True)
    return tuple(jnp.stack([a, c]) for a, c in zip(f, b))


def _chunk_colors(nk):
    n_main = (nk // 3) * 3
    if n_main == 0:
        return tuple(range(nk))
    return tuple(c % 3 if c < n_main else 3 + c - n_main for c in range(nk))


def _flash_kernel(qT_ref, k_ref, vT_ref, o_ref, m_ref, acc_ref, *buf_refs, tq, nq, tk, nk, sub, n_ctx):
    lag = FLASH_LAG
    colors = _chunk_colors(nk)
    n_buf = max(colors) + 1
    s_bufs, c_bufs = buf_refs[:n_buf], buf_refs[n_buf:]
    n_sub = tk // sub
    n_loop = max(((nk // 3) * 3 - lag) // 3, 0)

    def key_rows(c, r):
        base = c * tk if isinstance(c, int) else pl.multiple_of(c * tk, tk)
        return pl.ds(base + r * sub, sub)

    def q_cols(i):
        return pl.ds(i * tq if isinstance(i, int) else pl.multiple_of(i * tq, tq), tq)

    def tick(consume, score):
        if consume is not None:
            c, buf = consume
            m_prev = m_ref[...]
            m_new = jnp.maximum(m_prev, c_bufs[buf][...])
            alpha = jnp.exp2(m_prev - m_new)
            m_ref[...] = m_new
        cmax = pv = None
        for r in range(n_sub):
            rows = pl.ds(r * sub, sub)
            if consume is not None:
                p = jnp.exp2(s_bufs[buf][rows, :] - m_new).astype(BF16)
                part = _dot(vT_ref[0, :, key_rows(c, r)], p)
                pv = part if pv is None else pv + part
            if score is not None:
                s = _dot(k_ref[key_rows(score[0], r), :], score[2])
                s_bufs[score[1]][rows, :] = s
                part = jnp.max(s, axis=0, keepdims=True)
                cmax = part if cmax is None else jnp.maximum(cmax, part)
        if score is not None:
            c_bufs[score[1]][...] = cmax
        if consume is not None:
            acc_ref[...] = alpha * acc_ref[...] + pv

    def q_tile(i, carry, has_next=True):
        q = qT_ref[0, :, q_cols(i)]
        q_next = qT_ref[0, :, q_cols(i + 1)] if has_next else None
        m_ref[...] = jnp.full_like(m_ref, NEG_BIG)
        acc_ref[...] = jnp.zeros_like(acc_ref)

        def body(t, carry):
            for u in range(3):
                tick((t * 3 + u, u), (t * 3 + u + lag, (u + lag) % 3, q))
            return carry

        if n_loop:
            lax.fori_loop(0, n_loop, body, 0)
        for c in range(3 * n_loop, nk):
            ahead = c + lag
            if ahead < nk:
                score = (ahead, colors[ahead], q)
            elif has_next and ahead - nk < min(lag, nk):
                score = (ahead - nk, colors[ahead - nk], q_next)
            else:
                score = None
            tick((c, colors[c]), score)
        acc = acc_ref[...]
        o_ref[q_cols(i), :] = (acc[:HEAD_LANES] / acc[HEAD_LANES:HEAD_LANES + 1]).T.astype(o_ref.dtype)
        return carry

    q0 = qT_ref[0, :, q_cols(0)]
    for c in range(min(lag, nk)):
        tick(None, (c, colors[c], q0))
    if nq > 1:
        lax.fori_loop(0, nq - 1, q_tile, 0)
    q_tile(nq - 1, 0, has_next=False)

    ctx = pl.ds(nq * tq, n_ctx)
    s = _dot(k_ref[ctx, :], qT_ref[0, :, ctx])
    p = jnp.exp2(s - jnp.max(s, axis=0, keepdims=True)).astype(BF16)
    pv = _dot(vT_ref[0, :, ctx], p)
    o_ref[ctx, :] = (pv[:HEAD_LANES] / pv[HEAD_LANES:HEAD_LANES + 1]).T.astype(o_ref.dtype)


def _pick_tile(n, candidates):
    for t in candidates:
        if n % t == 0:
            return t
    raise ValueError(f"no tile for {n}")


def _attention(qT, k, vT, group, L, Cn, name):
    hq, T = qT.shape[0], qT.shape[2]
    tq = _pick_tile(L, (1024, 512, 256, 128))
    tk = _pick_tile(T, (1280, 256, 128))
    sub = min(tk, FLASH_SUB)
    assert T == L + Cn and tk % sub == 0 and Cn % HEAD_LANES == 0
    nq, nk = L // tq, T // tk
    n_buf = max(_chunk_colors(nk)) + 1
    return pl.pallas_call(
        functools.partial(_flash_kernel, tq=tq, nq=nq, tk=tk, nk=nk, sub=sub, n_ctx=Cn),
        out_shape=jax.ShapeDtypeStruct((T, hq * HEAD_LANES), BF16),
        grid=(hq,),
        in_specs=[
            pl.BlockSpec((1, HEAD_LANES, T), lambda h: (h, 0, 0), pipeline_mode=pl.Buffered(1)),
            pl.BlockSpec((T, HEAD_LANES), lambda h: (0, h // group)),
            pl.BlockSpec((1, V_ROWS, T), lambda h: (h // group, 0, 0)),
        ],
        out_specs=pl.BlockSpec((T, HEAD_LANES), lambda h: (0, h)),
        scratch_shapes=[pltpu.VMEM((1, tq), F32), pltpu.VMEM((V_ROWS, tq), F32)]
                       + [pltpu.VMEM((tk, tq), F32)] * n_buf + [pltpu.VMEM((1, tq), F32)] * n_buf,
        compiler_params=_params("arbitrary"),
        name=name,
    )(qT, k, vT)


def _mix_ffn_kernel(x_ref, xc_ref, mod_ref, gmix_ref, yf_ref, yb_ref, rg_ref, ym_ref, yg_ref,
                    w_gates, w_ro, w_mo, w_go, w_out, gffn_ref, w_a, w_b, w_o, gf_ref, o_ref,
                    *, final_norm, n_lat_tiles):
    x = _stream_tile(x_ref, xc_ref, n_lat_tiles)
    mod = mod_ref[0]
    hb = _prenorm(x, mod, gmix_ref[...], 0, 1).astype(BF16)
    gs = jax.nn.sigmoid(_dot(hb, w_gates[...]))
    y = yf_ref[...] + yb_ref[...]
    g = _silu(rg_ref[...])
    parts = []
    for h in range(RET_HEADS):
        sl = slice(h * RET_DV, (h + 1) * RET_DV)
        parts.append((g[:, sl] * _rms(y[:, sl])).astype(BF16))
    y_ret = jnp.concatenate(parts, axis=-1)
    D = D_MODEL
    z = (gs[:, :D] * _dot(y_ret, w_ro[...])
         + gs[:, D:2 * D] * _dot(ym_ref[...], w_mo[...])
         + gs[:, 2 * D:] * _dot(yg_ref[...], w_go[...]))
    x = x + mod[2:3] * _dot(z.astype(BF16), w_out[...])

    hb = _prenorm(x, mod, gffn_ref[...], 3, 4).astype(BF16)
    hidden = w_o.shape[0]
    cut = (hidden // 2 + FFN_SLAB_ALIGN - 1) // FFN_SLAB_ALIGN * FFN_SLAB_ALIGN
    y_ffn = None
    for lo, hi in ((0, cut), (cut, hidden)):
        u = (_silu(_dot(hb, w_a[:, lo:hi])) * _dot(hb, w_b[:, lo:hi])).astype(BF16)
        part = _dot(u, w_o[lo:hi, :])
        y_ffn = part if y_ffn is None else y_ffn + part
    x = x + mod[5:6] * y_ffn
    o_ref[...] = _rms(x) * gf_ref[...] if final_norm else x


def _mix_ffn_call(stream, mods, gmix, yf, yb, rg, ym, yg, merge_w, g_ffn, ffn_w, g_final, n_lat_tiles, l,
                  rows, final_norm):
    D = D_MODEL
    tm = ROW_TILE
    row = lambda w: pl.BlockSpec((tm, w), lambda i: (i, 0))
    once = lambda w: _layer_spec(w, l, pl.Buffered(1))
    return pl.pallas_call(
        functools.partial(_mix_ffn_kernel, final_norm=final_norm, n_lat_tiles=n_lat_tiles),
        out_shape=jax.ShapeDtypeStruct((rows, D), F32),
        grid=(rows // tm,),
        in_specs=_stream_specs(stream, n_lat_tiles) + [_mod_spec(l, n_lat_tiles), _layer_spec(gmix, l),
                  row(RET_W), row(RET_W), row(RET_W), row(MLA_W), row(GQA_W)]
                 + [once(w) for w in merge_w] + [_layer_spec(g_ffn, l)] + [once(w) for w in ffn_w]
                 + [_const_spec((1, D))],
        out_specs=row(D),
        compiler_params=_params("arbitrary"),
        name="mix_ffn",
    )(stream[0], stream[1], mods, gmix, yf, yb, rg, ym, yg, *merge_w, g_ffn, *ffn_w, g_final)


def _rope_tables(L, Cn):
    n_rows = L // GRID_W

    def angles(n, dim):
        half = dim // 2
        inv_freq = ROPE_BASE ** (-jnp.arange(half, dtype=F32) / half)
        ang = jnp.arange(n, dtype=jnp.int32).astype(F32)[:, None] * inv_freq[None, :]
        return jnp.cos(ang), jnp.sin(ang)

    def tables(dim, lane0):
        cr, sr = (jnp.repeat(a, GRID_W, axis=0) for a in angles(n_rows, dim // 2))
        cc, sc = (jnp.tile(a, (n_rows, 1)) for a in angles(GRID_W, dim // 2))
        z = jnp.zeros_like(sr)
        c = jnp.concatenate([cr, cr, cc, cc], axis=1)
        sa = jnp.concatenate([-sr, z, -sc, z], axis=1)
        sb = jnp.concatenate([z, sr, z, sc], axis=1)
        pad = lambda a, fill: jnp.concatenate(
            [jnp.full((L, lane0), fill, F32), a, jnp.full((L, HEAD_LANES - lane0 - dim), fill, F32)], axis=1)
        c, sa, sb = pad(c, 1.0), pad(sa, 0.0), pad(sb, 0.0)
        ctx = lambda fill: jnp.full((Cn, HEAD_LANES), fill, F32)
        return (jnp.concatenate([c, ctx(1.0)]), jnp.concatenate([sa, ctx(0.0)]),
                jnp.concatenate([sb, ctx(0.0)]))

    return tables(RET_DK, 0) + tables(MLA_ROPE, MLA_NOPE)


def _stacked_weights(w_in, w_mla_qb, w_mla_kvb):
    depth = w_in.shape[0]
    split_at = np.cumsum(IN_SIZES)[:-1].tolist()
    rq, rk, rv, rg, cq, ckv, kr, gq, gk, gv, gates = (
        w.astype(BF16) for w in jnp.split(w_in, split_at, axis=-1))
    kr_pad = jnp.pad(kr, ((0, 0), (0, 0), (MLA_NOPE, HEAD_LANES - MLA_NOPE - MLA_ROPE)))
    qb = w_mla_qb.astype(BF16).reshape(depth, MLA_Q_RANK, MLA_HEADS, MLA_NOPE + MLA_ROPE)
    qb = jnp.pad(qb, ((0, 0), (0, 0), (0, 0), (0, HEAD_LANES - MLA_NOPE - MLA_ROPE)))
    kvb = w_mla_kvb.astype(BF16).reshape(depth, MLA_KV_RANK, MLA_HEADS, MLA_NOPE + MLA_V)
    kb = jnp.pad(kvb[..., :MLA_NOPE], ((0, 0), (0, 0), (0, 0), (0, HEAD_LANES - MLA_NOPE)))
    flat = lambda w: w.reshape(depth, w.shape[1], -1)
    wT = lambda w: jnp.swapaxes(w, 1, 2)
    natural = (rq, rk, rv, rg, ckv, kr_pad, gk, flat(kb))
    transposed = (wT(cq), wT(flat(qb)), wT(flat(kvb[..., MLA_NOPE:])), wT(gq), wT(gv))
    return natural, transposed, gates


def kernel(x, c, ctx, c_ctx, w_mod, b_mod, g_mix, w_in, g_mla_q, g_mla_kv, w_mla_qb, w_mla_kvb,
           g_gqa_q, g_gqa_k, w_ret_o, w_mla_o, w_gqa_o, w_out, g_ffn, w_ffn_in, w_ffn_out, g_final):
    B, L, D = x.shape
    Cn = ctx.shape[1]
    depth = w_mod.shape[0]
    assert B == 1 and D == D_MODEL and L % ROW_TILE == 0 and Cn % ROW_TILE == 0 and L % Cn == 0
    assert L % GRID_W == 0
    n_lat_tiles = L // ROW_TILE
    ffn_hidden = w_ffn_out.shape[1]

    stream = (x[0], ctx[0], 0)
    cond = jnp.zeros((MOD_ROWS, D), F32).at[0].set(c[0]).at[1].set(c_ctx)
    mods = _mod_call(cond, w_mod, b_mod)
    mods = mods[:, :2].reshape(depth, 2, N_MOD, D)
    mods = jnp.pad(mods, ((0, 0), (0, 0), (0, MOD_ROWS - N_MOD), (0, 0)))

    tabs = _rope_tables(L, Cn)
    ret_consts = _ret_consts()
    rows3 = lambda g: g.reshape(depth, 1, -1)
    w_nat, w_tr, w_gates = _stacked_weights(w_in, w_mla_qb, w_mla_kvb)
    lanes = lambda g: jnp.broadcast_to(g[:, :, None], g.shape + (ROW_TILE,))
    wts = w_nat + w_tr + (rows3(g_mla_kv), rows3(g_gqa_k), lanes(g_mla_q), lanes(g_gqa_q))
    tabsT = tuple(t.T for t in tabs)
    merge_w = (w_gates,) + tuple(w.astype(BF16) for w in (w_ret_o, w_mla_o, w_gqa_o, w_out))
    w_ffn_a = w_ffn_in[..., :ffn_hidden].astype(BF16)
    w_ffn_b = w_ffn_in[..., ffn_hidden:].astype(BF16)
    w_ffn_o = w_ffn_out.astype(BF16)
    g_mix3, g_ffn3 = rows3(g_mix), rows3(g_ffn)

    for l in range(depth):
        rq, rk, rv, rg, mqT, mk, mvT, gqT, gk, gvT = _inproj_call(
            stream, mods, g_mix3, tabs, tabsT, wts, n_lat_tiles, l)
        yf, yb = _ret_call(rq, rk, rv, ret_consts, L // RET_CHUNK, Cn // RET_CHUNK)
        ym = _attention(mqT, mk, mvT, 1, L, Cn, "mla")
        yg = _attention(gqT, gk, gvT, GQA_HEADS // GQA_KV_HEADS, L, Cn, "gqa")
        last = l == depth - 1
        xs = _mix_ffn_call(stream, mods, g_mix3, yf, yb, rg, ym, yg, merge_w, g_ffn3,
                           (w_ffn_a, w_ffn_b, w_ffn_o), g_final.reshape(1, -1), n_lat_tiles, l,
                           rows=L if last else L + Cn, final_norm=last)
        stream = (xs, xs, n_lat_tiles)
    return xs[None]
```
